```python
import jax, jax.numpy as jnp
from jax import lax
import numpy as np

D_MODEL = 2048
BATCH = 8
SEQ = 2048
DEPTH = 4

HEAD_DIM = 128
DIL_GROUPS = ((128, 1), (512, 4), (2048, 16))
HEADS_PER_GROUP = 4
N_ATTN_HEADS = HEADS_PER_GROUP * len(DIL_GROUPS)
ATTN_WIDTH = N_ATTN_HEADS * HEAD_DIM
ATTN_OUT_WIDTH = HEADS_PER_GROUP * HEAD_DIM
BAND_BLOCK = 64
NEG_INF = -1e30
CONV_WIDTH = D_MODEL // 2
CONV_KERNEL = 31
N_MEM = 256
X_HEADS = 4
X_HEAD_DIM = D_MODEL // X_HEADS
D_FF = -(-8 * D_MODEL // (3 * 256)) * 256
IN_WIDTH = 2 * CONV_WIDTH + 3 * ATTN_WIDTH + 2 * D_MODEL
EPS = 1e-6

kernel_name = "hybrid_conformer_dilated_encoder"


def rms_norm(x, g):
    xf = x.astype(jnp.float32)
    y = xf * lax.rsqrt(jnp.mean(xf * xf, axis=-1, keepdims=True) + EPS)
    return (y * g.astype(jnp.float32)).astype(x.dtype)


def layer_norm(x, g, b):
    xf = x.astype(jnp.float32)
    mu = jnp.mean(xf, axis=-1, keepdims=True)
    var = jnp.mean(jnp.square(xf - mu), axis=-1, keepdims=True)
    y = (xf - mu) * lax.rsqrt(var + EPS)
    return (y * g.astype(jnp.float32) + b.astype(jnp.float32)).astype(x.dtype)


def alibi_slopes(n):
    return jnp.exp2(-8.0 * (jnp.arange(n, dtype=jnp.float32) + 1.0) / n)


def conformer_conv_branch(u, w_dw, b_dw, ln_g, ln_b):
    a, gate = jnp.split(u, 2, axis=-1)
    h = a * jax.nn.sigmoid(gate)
    pad = CONV_KERNEL // 2
    h = lax.conv_general_dilated(
        h, w_dw[:, None, :].astype(h.dtype), window_strides=(1,), padding=[(pad, pad)],
        dimension_numbers=("NWC", "WIO", "NWC"), feature_group_count=h.shape[-1]) + b_dw
    h = layer_norm(h, ln_g, ln_b)
    return jax.nn.silu(h)


def dilated_window_attention(q, k, v, window, dilation, slopes):
    B, S, H, Dh = q.shape
    half = window // (2 * dilation)
    L = S // dilation
    nb = -(-L // BAND_BLOCK)
    Lp = nb * BAND_BLOCK

    def to_sub(t):
        return t.reshape(B, L, dilation, H, Dh).transpose(0, 2, 3, 1, 4).astype(jnp.float32)

    qs = jnp.pad(to_sub(q), ((0, 0), (0, 0), (0, 0), (0, Lp - L), (0, 0)))
    kv_pad = ((0, 0), (0, 0), (0, 0), (BAND_BLOCK, Lp - L + BAND_BLOCK), (0, 0))
    kp = jnp.pad(to_sub(k), kv_pad)
    vp = jnp.pad(to_sub(v), kv_pad)
    qb = qs.reshape(B, dilation, H, nb, BAND_BLOCK, Dh)

    def band(t):
        return jnp.concatenate(
            [t[:, :, :, o:o + Lp].reshape(B, dilation, H, nb, BAND_BLOCK, Dh)
             for o in (0, BAND_BLOCK, 2 * BAND_BLOCK)], axis=4)

    kb = band(kp)
    vb = band(vp)
    jq = jnp.arange(Lp).reshape(nb, BAND_BLOCK)
    jk = (jnp.arange(nb)[:, None] * BAND_BLOCK
          + jnp.arange(3 * BAND_BLOCK)[None, :] - BAND_BLOCK)
    rel = jk[:, None, :] - jq[:, :, None]
    valid = (jnp.abs(rel) <= half) & (jk[:, None, :] >= 0) & (jk[:, None, :] < L)
    dist = (dilation * jnp.abs(rel)).astype(jnp.float32)

    s = jnp.einsum("brhnqd,brhnkd->brhnqk", qb, kb) * (Dh ** -0.5)
    s = s - slopes[:, None, None, None] * dist
    s = jnp.where(valid, s, NEG_INF)
    m = jnp.max(s, axis=-1, keepdims=True)
    p = jnp.exp(s - m)
    den = jnp.sum(p, axis=-1, keepdims=True)
    o = jnp.einsum("brhnqk,brhnkd->brhnqd", p, vb) / den
    lse = (m + jnp.log(den))[..., 0]

    o = o.reshape(B, dilation, H, Lp, Dh)[:, :, :, :L].transpose(0, 3, 1, 2, 4).reshape(B, S, H, Dh)
    lse = lse.reshape(B, dilation, H, Lp)[:, :, :, :L].transpose(0, 3, 1, 2).reshape(B, S, H)
    return o, lse


def dilated_mixture_branch(q, k, v):
    B, S, _ = q.shape
    q = q.reshape(B, S, N_ATTN_HEADS, HEAD_DIM)
    k = k.reshape(B, S, N_ATTN_HEADS, HEAD_DIM)
    v = v.reshape(B, S, N_ATTN_HEADS, HEAD_DIM)
    slopes = alibi_slopes(N_ATTN_HEADS)
    outs, lses = [], []
    for g, (window, dilation) in enumerate(DIL_GROUPS):
        hs = slice(g * HEADS_PER_GROUP, (g + 1) * HEADS_PER_GROUP)
        o, lse = dilated_window_attention(q[:, :, hs], k[:, :, hs], v[:, :, hs],
                                          window, dilation, slopes[hs])
        outs.append(o)
        lses.append(lse)
    alpha = jax.nn.softmax(jnp.stack(lses, axis=0), axis=0)
    y = jnp.sum(alpha[..., None] * jnp.stack(outs, axis=0), axis=0)
    return y.reshape(B, S, ATTN_OUT_WIDTH).astype(v.dtype)


def memory_cross_attention(h, memn, w_cq, w_ck, w_cv, w_co):
    B, S, _ = h.shape
    M = memn.shape[1]
    q = (h @ w_cq).reshape(B, S, X_HEADS, X_HEAD_DIM)
    k = (memn @ w_ck).reshape(B, M, X_HEADS, X_HEAD_DIM)
    v = (memn @ w_cv).reshape(B, M, X_HEADS, X_HEAD_DIM)
    s = jnp.einsum("bshd,bmhd->bhsm", q.astype(jnp.float32), k.astype(jnp.float32)) * (X_HEAD_DIM ** -0.5)
    p = jax.nn.softmax(s, axis=-1)
    o = jnp.einsum("bhsm,bmhd->bshd", p, v.astype(jnp.float32)).astype(h.dtype)
    return o.reshape(B, S, D_MODEL) @ w_co


def swiglu_ffn(h, w_gate, w_up, w_down):
    return (jax.nn.silu(h @ w_gate) * (h @ w_up)) @ w_down


def setup_inputs(seed: int = 0) -> dict:
    key = jax.random.key(seed)
    ks = jax.random.split(key, 24)
    f32 = jnp.float32

    def w(k, shape, fan_in):
        return jax.random.normal(k, shape, f32) * (fan_in ** -0.5)

    def gain(k, shape):
        return 1.0 + 0.02 * jax.random.normal(k, shape, f32)

    def bias(k, shape):
        return 0.02 * jax.random.normal(k, shape, f32)

    L = DEPTH
    return {
        "x": jax.random.normal(ks[0], (BATCH, SEQ, D_MODEL), f32),
        "mem": jax.random.normal(ks[1], (BATCH, N_MEM, D_MODEL), f32),
        "w_in": w(ks[2], (L, D_MODEL, IN_WIDTH), D_MODEL),
        "w_dw": w(ks[3], (L, CONV_KERNEL, CONV_WIDTH), CONV_KERNEL),
        "b_dw": bias(ks[4], (L, CONV_WIDTH)),
        "conv_ln_g": gain(ks[5], (L, CONV_WIDTH)),
        "conv_ln_b": bias(ks[6], (L, CONV_WIDTH)),
        "w_conv_out": w(ks[7], (L, CONV_WIDTH, D_MODEL), CONV_WIDTH),
        "w_attn_proj": w(ks[8], (L, ATTN_OUT_WIDTH, D_MODEL), ATTN_OUT_WIDTH),
        "w_o": w(ks[9], (L, D_MODEL, D_MODEL), D_MODEL),
        "g_mix_pre": gain(ks[10], (L, D_MODEL)),
        "g_mix_post": gain(ks[11], (L, D_MODEL)),
        "g_mem": gain(ks[12], (L, D_MODEL)),
        "w_cq": w(ks[13], (L, D_MODEL, D_MODEL), D_MODEL),
        "w_ck": w(ks[14], (L, D_MODEL, D_MODEL), D_MODEL),
        "w_cv": w(ks[15], (L, D_MODEL, D_MODEL), D_MODEL),
        "w_co": w(ks[16], (L, D_MODEL, D_MODEL), D_MODEL),
        "g_x_pre": gain(ks[17], (L, D_MODEL)),
        "g_x_post": gain(ks[18], (L, D_MODEL)),
        "w_ffn_gate": w(ks[19], (L, D_MODEL, D_FF), D_MODEL),
        "w_ffn_up": w(ks[20], (L, D_MODEL, D_FF), D_MODEL),
        "w_ffn_down": w(ks[21], (L, D_FF, D_MODEL), D_FF),
        "g_ffn_pre": gain(ks[22], (L, D_MODEL)),
        "g_ffn_post": gain(ks[23], (L, D_MODEL)),
    }


def reference(x, mem, w_in, w_dw, b_dw, conv_ln_g, conv_ln_b, w_conv_out, w_attn_proj, w_o,
              g_mix_pre, g_mix_post, g_mem, w_cq, w_ck, w_cv, w_co, g_x_pre, g_x_post,
              w_ffn_gate, w_ffn_up, w_ffn_down, g_ffn_pre, g_ffn_post):
    c0 = 2 * CONV_WIDTH
    c1 = c0 + ATTN_WIDTH
    c2 = c1 + ATTN_WIDTH
    c3 = c2 + ATTN_WIDTH
    c4 = c3 + D_MODEL
    for l in range(DEPTH):
        h = rms_norm(x, g_mix_pre[l])
        z = h @ w_in[l]
        u_conv, q, k, v, ga, gb = jnp.split(z, [c0, c1, c2, c3, c4], axis=-1)
        y_conv = conformer_conv_branch(u_conv, w_dw[l], b_dw[l], conv_ln_g[l], conv_ln_b[l]) @ w_conv_out[l]
        y_attn = dilated_mixture_branch(q, k, v) @ w_attn_proj[l]
        merged = jax.nn.sigmoid(ga) * y_conv + jax.nn.sigmoid(gb) * y_attn
        x = x + rms_norm(merged @ w_o[l], g_mix_post[l])
        h = rms_norm(x, g_x_pre[l])
        memn = rms_norm(mem, g_mem[l])
        x = x + rms_norm(memory_cross_attention(h, memn, w_cq[l], w_ck[l], w_cv[l], w_co[l]), g_x_post[l])
        h = rms_norm(x, g_ffn_pre[l])
        x = x + rms_norm(swiglu_ffn(h, w_ffn_gate[l], w_ffn_up[l], w_ffn_down[l]), g_ffn_post[l])
    return x
```

```python
import functools

import jax
import jax.numpy as jnp
from jax import lax
from jax.experimental import pallas as pl
from jax.experimental.pallas import tpu as pltpu

F32 = jnp.float32
BF16 = jnp.bfloat16

EPS = 1e-6
NEG_INF = -1e30

HEAD_DIM = 128
DIL_GROUPS = ((128, 1), (512, 4), (2048, 16))
HEADS_PER_GROUP = 4
N_ATTN_HEADS = HEADS_PER_GROUP * len(DIL_GROUPS)
BAND_HALF = 64
X_HEADS = 4

V7X_VMEM_LIMIT_BYTES = 56 * 1024 * 1024


def _params(n_grid_dims):
    return pltpu.CompilerParams(
        dimension_semantics=("arbitrary",) * n_grid_dims,
        vmem_limit_bytes=V7X_VMEM_LIMIT_BYTES)


def _resident(shape):
    return pl.BlockSpec(shape, lambda *_: (0,) * len(shape), pipeline_mode=pl.Buffered(1))


def _rms_rows_to_bf16(x_ref, g_ref, h_ref, rows, chunk=64):
    def body(i, carry):
        r = pl.multiple_of(i * chunk, chunk)
        x = x_ref[pl.ds(r, chunk), :]
        ms = jnp.mean(x * x, axis=-1, keepdims=True)
        h_ref[pl.ds(r, chunk), :] = (x * lax.rsqrt(ms + EPS) * g_ref[...]).astype(BF16)
        return carry
    lax.fori_loop(0, rows // chunk, body, 0)


def _rms(y, g):
    ms = jnp.mean(y * y, axis=-1, keepdims=True)
    return y * lax.rsqrt(ms + EPS) * g


def _norm_matmul_kernel(x_ref, g_ref, w_ref, cs_ref, o_ref, h_ref, *, bm, act):
    @pl.when(pl.program_id(1) == 0)
    def _():
        _rms_rows_to_bf16(x_ref, g_ref, h_ref, bm)

    acc = jnp.dot(h_ref[...], w_ref[...], preferred_element_type=F32)
    if act == "sigmoid":
        acc = jax.nn.sigmoid(acc)
    elif act == "colscale":
        acc = acc * cs_ref[...]
    o_ref[...] = acc.astype(o_ref.dtype)


def norm_matmul(x, g, w, *, bm, bn, act="none", colscale=None):
    M, K = x.shape
    N = w.shape[1]
    assert M % bm == 0 and N % bn == 0
    if colscale is None:
        colscale = jnp.ones((N,), F32)
    return pl.pallas_call(
        functools.partial(_norm_matmul_kernel, bm=bm, act=act),
        grid=(M // bm, N // bn),
        in_specs=[
            pl.BlockSpec((bm, K), lambda i, j: (i, 0)),
            pl.BlockSpec((1, K), lambda i, j: (0, 0)),
            pl.BlockSpec((K, bn), lambda i, j: (0, j)),
            pl.BlockSpec((1, bn), lambda i, j: (0, j)),
        ],
        out_specs=pl.BlockSpec((bm, bn), lambda i, j: (i, j)),
        out_shape=jax.ShapeDtypeStruct((M, N), BF16),
        scratch_shapes=[pltpu.VMEM((bm, K), BF16)],
        compiler_params=_params(2),
        name="norm_matmul_" + act,
    )(x, g.reshape(1, K), w, colscale.reshape(1, N))


def _norm_gated_kernel(x_ref, g_ref, wa_ref, wb_ref, o_ref, h_ref, *, bm, mode):
    @pl.when(pl.program_id(1) == 0)
    def _():
        _rms_rows_to_bf16(x_ref, g_ref, h_ref, bm)

    h = h_ref[...]
    a = jnp.dot(h, wa_ref[...], preferred_element_type=F32)
    b = jnp.dot(h, wb_ref[...], preferred_element_type=F32)
    if mode == "glu":
        y = a * jax.nn.sigmoid(b)
    else:
        y = a * jax.nn.sigmoid(a) * b
    o_ref[...] = y.astype(o_ref.dtype)


def norm_gated_matmul(x, g, wa, wb, *, bm, bn, mode):
    M, K = x.shape
    N = wa.shape[1]
    assert M % bm == 0 and N % bn == 0
    return pl.pallas_call(
        functools.partial(_norm_gated_kernel, bm=bm, mode=mode),
        grid=(M // bm, N // bn),
        in_specs=[
            pl.BlockSpec((bm, K), lambda i, j: (i, 0)),
            pl.BlockSpec((1, K), lambda i, j: (0, 0)),
            pl.BlockSpec((K, bn), lambda i, j: (0, j)),
            pl.BlockSpec((K, bn), lambda i, j: (0, j)),
        ],
        out_specs=pl.BlockSpec((bm, bn), lambda i, j: (i, j)),
        out_shape=jax.ShapeDtypeStruct((M, N), BF16),
        scratch_shapes=[pltpu.VMEM((bm, K), BF16)],
        compiler_params=_params(2),
        name="norm_gated_" + mode,
    )(x, g.reshape(1, K), wa, wb)


def _matmul_norm_res_kernel(a_ref, w_ref, x_ref, g_ref, o_ref, acc_ref):
    k = pl.program_id(1)
    part = jnp.dot(a_ref[...], w_ref[...], preferred_element_type=F32)

    @pl.when(k == 0)
    def _():
        acc_ref[...] = part

    @pl.when(k > 0)
    def _():
        acc_ref[...] += part

    @pl.when(k == pl.num_programs(1) - 1)
    def _():
        o_ref[...] = x_ref[...] + _rms(acc_ref[...], g_ref[...])


def matmul_norm_residual(a, w, x, g, *, bm, bk):
    M, K = a.shape
    N = w.shape[1]
    assert M % bm == 0 and K % bk == 0
    return pl.pallas_call(
        _matmul_norm_res_kernel,
        grid=(M // bm, K // bk),
        in_specs=[
            pl.BlockSpec((bm, bk), lambda i, k: (i, k)),
            pl.BlockSpec((bk, N), lambda i, k: (k, 0)),
            pl.BlockSpec((bm, N), lambda i, k: (i, 0)),
            pl.BlockSpec((1, N), lambda i, k: (0, 0)),
        ],
        out_specs=pl.BlockSpec((bm, N), lambda i, k: (i, 0)),
        out_shape=jax.ShapeDtypeStruct((M, N), F32),
        scratch_shapes=[pltpu.VMEM((bm, N), F32)],
        compiler_params=_params(2),
        name="matmul_norm_residual",
    )(a, w, x, g.reshape(1, N))


LANES = 128
CONV_ROWS = 128
CONV_LN_ROWS = 32
CONV_DATA_OFF = 16


def _conv_kernel(h_ref, wdw_ref, bdw_ref, lng_ref, lnb_ref, o_ref, hp_ref, cv_ref, *, S, C, KW):
    pad = KW // 2
    off = CONV_DATA_OFF
    n_slab = C // LANES
    R = CONV_ROWS

    for j in range(n_slab):
        cols = slice(j * LANES, (j + 1) * LANES)
        hp_ref[j, 0:off, :] = jnp.zeros((off, LANES), F32)
        hp_ref[j, off + S:off + S + off, :] = jnp.zeros((off, LANES), F32)

        def copy(i, carry, j=j, cols=cols):
            r = pl.multiple_of(i * 256, 256)
            hp_ref[j, pl.ds(off + r, 256), :] = h_ref[pl.ds(r, 256), cols].astype(F32)
            return carry
        lax.fori_loop(0, S // 256, copy, 0)

        def block(i, carry, j=j, cols=cols):
            for phase in range(2):
                t0 = i * (2 * R) + phase
                acc = jnp.broadcast_to(bdw_ref[:, cols], (R, LANES))
                for k in range(KW):
                    taps = hp_ref[j, pl.ds(t0 + (off - pad + k), R, stride=2), :]
                    acc = acc + taps * wdw_ref[k:k + 1, cols]
                cv_ref[j, pl.ds(t0, R, stride=2), :] = acc
            return carry
        lax.fori_loop(0, S // (2 * R), block, 0)

    def norm(i, carry):
        rows = pl.ds(pl.multiple_of(i * CONV_LN_ROWS, CONV_LN_ROWS), CONV_LN_ROWS)
        xs = [cv_ref[j, rows, :] for j in range(n_slab)]
        mu = sum(jnp.sum(x, axis=-1, keepdims=True) for x in xs) * (1.0 / C)
        xs = [x - mu for x in xs]
        var = sum(jnp.sum(x * x, axis=-1, keepdims=True) for x in xs) * (1.0 / C)
        inv = lax.rsqrt(var + EPS)
        for j in range(n_slab):
            cols = slice(j * LANES, (j + 1) * LANES)
            y = xs[j] * inv * lng_ref[:, cols] + lnb_ref[:, cols]
            o_ref[rows, cols] = (y * jax.nn.sigmoid(y)).astype(o_ref.dtype)
        return carry
    lax.fori_loop(0, S // CONV_LN_ROWS, norm, 0)


def conv_branch(h, w_dw, b_dw, ln_g, ln_b):
    B, S, C = h.shape
    KW = w_dw.shape[0]
    assert KW // 2 < CONV_DATA_OFF and S % (2 * CONV_ROWS) == 0 and C % LANES == 0
    row = lambda v: v.reshape(1, C)
    vec = pl.BlockSpec((1, C), lambda b: (0, 0))
    return pl.pallas_call(
        functools.partial(_conv_kernel, S=S, C=C, KW=KW),
        grid=(B,),
        in_specs=[
            pl.BlockSpec((None, S, C), lambda b: (b, 0, 0)),
            pl.BlockSpec((KW, C), lambda b: (0, 0)),
            vec, vec, vec,
        ],
        out_specs=pl.BlockSpec((None, S, C), lambda b: (b, 0, 0)),
        out_shape=jax.ShapeDtypeStruct((B, S, C), BF16),
        scratch_shapes=[pltpu.VMEM((C // LANES, S + 2 * CONV_DATA_OFF, LANES), F32),
                        pltpu.VMEM((C // LANES, S, LANES), F32)],
        compiler_params=_params(1),
        name="conv_branch",
    )(h, w_dw, row(b_dw), row(ln_g), row(ln_b))


ATTN_Q_ROWS = 128


def _attn_kernel(q0, k0, v0, q1, k1, v1, q2, k2, v2, o_ref,
                 qf, kf, vf, qs, kp, vp, og0, og1, og2, lg0, lg1, lg2, bias_ref, *, S):
    T = ATTN_Q_ROWS
    HB = BAND_HALF
    W = T + 2 * HB
    D = HEAD_DIM
    qkv = ((q0, k0, v0), (q1, k1, v1), (q2, k2, v2))
    outs = ((og0, lg0), (og1, lg1), (og2, lg2))
    slot_f = jnp.full((T, W), pl.program_id(1), jnp.int32).astype(F32)

    kp[...] = jnp.zeros(kp.shape, BF16)
    vp[...] = jnp.zeros(vp.shape, BF16)

    a_idx = lax.broadcasted_iota(jnp.int32, (T, W), 0)
    c_idx = lax.broadcasted_iota(jnp.int32, (T, W), 1)
    rel = jnp.abs(c_idx - HB - a_idx)
    c_row = lax.broadcasted_iota(jnp.int32, (1, W), 1)

    for g in reversed(range(len(DIL_GROUPS))):
        _, d = DIL_GROUPS[g]
        L = S // d
        q_ref, k_ref, v_ref = qkv[g]
        og, lg = outs[g]

        slope = jnp.exp2(-8.0 * (slot_f + (HEADS_PER_GROUP * g + 1.0)) / N_ATTN_HEADS)
        bias_ref[...] = jnp.where(rel <= HB, -(slope * (d * rel).astype(F32)), NEG_INF)

        def block(i, carry, r, q_src, d=d, L=L, og=og, lg=lg):
            i0 = pl.multiple_of(i * T, T)
            qb = q_src[pl.ds(i0, T), :]
            kw = kp[pl.ds(i0, W), :]
            vw = vp[pl.ds(i0, W), :]
            s = lax.dot_general(qb, kw, (((1,), (1,)), ((), ())), preferred_element_type=F32)
            kidx = c_row + (i0 - HB)
            col_bias = jnp.where(kidx >= 0, jnp.where(kidx < L, 0.0, NEG_INF), NEG_INF)
            s = s + bias_ref[...] + col_bias
            m = jnp.max(s, axis=-1, keepdims=True)
            p = jnp.exp(s - m)
            den = jnp.sum(p, axis=-1, keepdims=True)
            o = jnp.dot(p.astype(BF16), vw, preferred_element_type=F32) / den
            lse = jnp.broadcast_to(m + jnp.log(den), (T, D))
            if d == 1:
                og[pl.ds(i0, T), :] = o
                lg[pl.ds(i0, T), :] = lse
            else:
                og[pl.ds(r + d * i0, T, stride=d), :] = o
                lg[pl.ds(r + d * i0, T, stride=d), :] = lse
            return carry

        if d == 1:
            def copy(i, carry):
                r0 = pl.multiple_of(i * 256, 256)
                kp[pl.ds(HB + r0, 256), :] = k_ref[pl.ds(r0, 256), :]
                vp[pl.ds(HB + r0, 256), :] = v_ref[pl.ds(r0, 256), :]
                return carry
            lax.fori_loop(0, S // 256, copy, 0)
            lax.fori_loop(0, L // T, functools.partial(block, r=0, q_src=q_ref), 0)
        else:
            def widen(i, carry):
                r0 = pl.multiple_of(i * 256, 256)
                qf[pl.ds(r0, 256), :] = q_ref[pl.ds(r0, 256), :].astype(F32)
                kf[pl.ds(r0, 256), :] = k_ref[pl.ds(r0, 256), :].astype(F32)
                vf[pl.ds(r0, 256), :] = v_ref[pl.ds(r0, 256), :].astype(F32)
                return carry
            lax.fori_loop(0, S // 256, widen, 0)

            def residue(r, carry, d=d, L=L, block=block):
                qs[0:L, :] = qf[pl.ds(r, L, stride=d), :].astype(BF16)
                kp[HB:HB + L, :] = kf[pl.ds(r, L, stride=d), :].astype(BF16)
                vp[HB:HB + L, :] = vf[pl.ds(r, L, stride=d), :].astype(BF16)
                lax.fori_loop(0, L // T, functools.partial(block, r=r, q_src=qs), 0)
                return carry
            lax.fori_loop(0, d, residue, 0)

    def mix(i, carry):
        r0 = pl.multiple_of(i * 256, 256)
        rows = pl.ds(r0, 256)
        l0, l1, l2 = lg0[rows, :], lg1[rows, :], lg2[rows, :]
        m = jnp.maximum(jnp.maximum(l0, l1), l2)
        w0, w1, w2 = jnp.exp(l0 - m), jnp.exp(l1 - m), jnp.exp(l2 - m)
        y = (w0 * og0[rows, :] + w1 * og1[rows, :] + w2 * og2[rows, :]) / (w0 + w1 + w2)
        o_ref[rows, :] = y.astype(o_ref.dtype)
        return carry
    lax.fori_loop(0, S // 256, mix, 0)


def dilated_mixture_attention(qkv):
    B, S, _ = qkv.shape
    D = HEAD_DIM
    T = ATTN_Q_ROWS
    assert all(S % (d * T) == 0 and w // (2 * d) == BAND_HALF for w, d in DIL_GROUPS)
    assert S % 256 == 0

    def head_spec(part, g):
        return pl.BlockSpec((None, S, D),
                            lambda b, j: (b, 0, part * N_ATTN_HEADS + HEADS_PER_GROUP * g + j))
    in_specs = [head_spec(part, g) for g in range(len(DIL_GROUPS)) for part in range(3)]
    seq_f32 = pltpu.VMEM((S, D), F32)
    return pl.pallas_call(
        functools.partial(_attn_kernel, S=S),
        grid=(B, HEADS_PER_GROUP),
        in_specs=in_specs,
        out_specs=pl.BlockSpec((None, S, D), lambda b, j: (b, 0, j)),
        out_shape=jax.ShapeDtypeStruct((B, S, HEADS_PER_GROUP * D), BF16),
        scratch_shapes=[
            seq_f32, seq_f32, seq_f32,
            pltpu.VMEM((S, D), BF16),
            pltpu.VMEM((S + 2 * BAND_HALF, D), BF16),
            pltpu.VMEM((S + 2 * BAND_HALF, D), BF16),
            seq_f32, seq_f32, seq_f32,
            seq_f32, seq_f32, seq_f32,
            pltpu.VMEM((T, T + 2 * BAND_HALF), F32),
        ],
        compiler_params=_params(2),
        name="dilated_attention",
    )(*([qkv] * 9))


def _merge_kernel(c_ref, a_ref, ga_ref, gb_ref, x_ref, wc_ref, wa_ref, wo_ref, g_ref, o_ref):
    yc = jnp.dot(c_ref[...], wc_ref[...], preferred_element_type=F32)
    ya = jnp.dot(a_ref[...], wa_ref[...], preferred_element_type=F32)
    merged = ga_ref[...].astype(F32) * yc + gb_ref[...].astype(F32) * ya
    y = jnp.dot(merged.astype(BF16), wo_ref[...], preferred_element_type=F32)
    o_ref[...] = x_ref[...] + _rms(y, g_ref[...])


def merge_out(c, a, gates, x, wc, wa, wo, g, *, bm):
    M, D = x.shape
    Cc, Ca = c.shape[1], a.shape[1]
    assert M % bm == 0
    return pl.pallas_call(
        _merge_kernel,
        grid=(M // bm,),
        in_specs=[
            pl.BlockSpec((bm, Cc), lambda i: (i, 0)),
            pl.BlockSpec((bm, Ca), lambda i: (i, 0)),
            pl.BlockSpec((bm, D), lambda i: (i, 0)),
            pl.BlockSpec((bm, D), lambda i: (i, 1)),
            pl.BlockSpec((bm, D), lambda i: (i, 0)),
            _resident((Cc, D)), _resident((Ca, D)), _resident((D, D)), _resident((1, D)),
        ],
        out_specs=pl.BlockSpec((bm, D), lambda i: (i, 0)),
        out_shape=jax.ShapeDtypeStruct((M, D), F32),
        compiler_params=_params(1),
        name="merge_out",
    )(c, a, gates, gates, x, wc, wa, wo, g.reshape(1, D))


def _cross_kernel(x_ref, k_ref, v_ref, wq_ref, wo_ref, gpre_ref, gpost_ref, o_ref, h_ref, q_ref, ctx_ref,
                  *, bm, D):
    dh = D // X_HEADS
    _rms_rows_to_bf16(x_ref, gpre_ref, h_ref, bm)
    q = jnp.dot(h_ref[...], wq_ref[...], preferred_element_type=F32)
    q_ref[...] = (q * (dh ** -0.5)).astype(BF16)
    for hd in range(X_HEADS):
        cols = slice(hd * dh, (hd + 1) * dh)
        s = lax.dot_general(q_ref[:, cols], k_ref[:, cols], (((1,), (1,)), ((), ())),
                            preferred_element_type=F32)
        m = jnp.max(s, axis=-1, keepdims=True)
        p = jnp.exp(s - m)
        den = jnp.sum(p, axis=-1, keepdims=True)
        ctx = jnp.dot(p.astype(BF16), v_ref[:, cols], preferred_element_type=F32) / den
        ctx_ref[:, cols] = ctx.astype(BF16)
    y = jnp.dot(ctx_ref[...], wo_ref[...], preferred_element_type=F32)
    o_ref[...] = x_ref[...] + _rms(y, gpost_ref[...])


def cross_attention_sublayer(x, kmem, vmem, wq, wo, g_pre, g_post, *, bm):
    B, S, D = x.shape
    Nm = kmem.shape[1]
    assert S % bm == 0 and bm % 64 == 0
    tile = pl.BlockSpec((None, bm, D), lambda b, i: (b, i, 0))
    mem_spec = pl.BlockSpec((None, Nm, D), lambda b, i: (b, 0, 0))
    return pl.pallas_call(
        functools.partial(_cross_kernel, bm=bm, D=D),
        grid=(B, S // bm),
        in_specs=[tile, mem_spec, mem_spec, _resident((D, D)), _resident((D, D)),
                  _resident((1, D)), _resident((1, D))],
        out_specs=tile,
        out_shape=jax.ShapeDtypeStruct((B, S, D), F32),
        scratch_shapes=[pltpu.VMEM((bm, D), BF16), pltpu.VMEM((bm, D), BF16), pltpu.VMEM((bm, D), BF16)],
        compiler_params=_params(2),
        name="cross_attention",
    )(x, kmem, vmem, wq, wo, g_pre.reshape(1, D), g_post.reshape(1, D))


def kernel(x, mem, w_in, w_dw, b_dw, conv_ln_g, conv_ln_b, w_conv_out, w_attn_proj, w_o, g_mix_pre, g_mix_post, g_mem, w_cq, w_ck, w_cv, w_co, g_x_pre, g_x_post, w_ffn_gate, w_ffn_up, w_ffn_down, g_ffn_pre, g_ffn_post):
    B, S, D = x.shape
    depth = w_in.shape[0]
    Cc = w_dw.shape[-1]
    Wq = N_ATTN_HEADS * HEAD_DIM
    Nm = mem.shape[1]
    c0 = 2 * Cc
    c3 = c0 + 3 * Wq
    M = B * S

    q_scale = jnp.concatenate([jnp.full((Wq,), HEAD_DIM ** -0.5, F32), jnp.ones((2 * Wq,), F32)])
    mem2 = mem.reshape(B * Nm, D)
    x2 = x.reshape(M, D)
    bf = lambda w: w.astype(BF16)

    for l in range(depth):
        glu = norm_gated_matmul(x2, g_mix_pre[l], bf(w_in[l, :, :Cc]), bf(w_in[l, :, Cc:c0]),
                                bm=1024, bn=512, mode="glu")
        qkv = norm_matmul(x2, g_mix_pre[l], bf(w_in[l, :, c0:c3]), bm=1024, bn=1536,
                          act="colscale", colscale=q_scale)
        gates = norm_matmul(x2, g_mix_pre[l], bf(w_in[l, :, c3:]), bm=1024, bn=1024, act="sigmoid")
        conv = conv_branch(glu.reshape(B, S, Cc), w_dw[l], b_dw[l], conv_ln_g[l], conv_ln_b[l])
        attn = dilated_mixture_attention(qkv.reshape(B, S, 3 * Wq))
        x2 = merge_out(conv.reshape(M, Cc), attn.reshape(M, -1), gates, x2,
                       bf(w_conv_out[l]), bf(w_attn_proj[l]), bf(w_o[l]), g_mix_post[l], bm=256)
        kmem = norm_matmul(mem2, g_mem[l], bf(w_ck[l]), bm=1024, bn=1024)
        vmem = norm_matmul(mem2, g_mem[l], bf(w_cv[l]), bm=1024, bn=1024)
        x2 = cross_attention_sublayer(x2.reshape(B, S, D), kmem.reshape(B, Nm, D), vmem.reshape(B, Nm, D),
                                      bf(w_cq[l]), bf(w_co[l]), g_x_pre[l], g_x_post[l], bm=256).reshape(M, D)
        act = norm_gated_matmul(x2, g_ffn_pre[l], bf(w_ffn_gate[l]), bf(w_ffn_up[l]),
                                bm=1024, bn=512, mode="swiglu")
        x2 = matmul_norm_residual(act, bf(w_ffn_down[l]), x2, g_ffn_post[l], bm=512, bk=1408)
    return x2.reshape(B, S, D)
```

```python
import functools

import jax
import jax.numpy as jnp
from jax import lax
from jax.experimental import pallas as pl
from jax.experimental.pallas import tpu as pltpu

F32 = jnp.float32
BF16 = jnp.bfloat16

EPS = 1e-6
NEG_INF = -1e30

HEAD_DIM = 128
DIL_GROUPS = ((128, 1), (512, 4), (2048, 16))
HEADS_PER_GROUP = 4
N_ATTN_HEADS = HEADS_PER_GROUP * len(DIL_GROUPS)
BAND_HALF = 64
X_HEADS = 4

V7X_VMEM_LIMIT_BYTES = 56 * 1024 * 1024


def _params(n_grid_dims):
    return pltpu.CompilerParams(
        dimension_semantics=("arbitrary",) * n_grid_dims,
        vmem_limit_bytes=V7X_VMEM_LIMIT_BYTES)


def _resident(shape):
    return pl.BlockSpec(shape, lambda *_: (0,) * len(shape), pipeline_mode=pl.Buffered(1))


def _rms_rows_to_bf16(x_ref, g_ref, h_ref, rows, chunk=64):
    def body(i, carry):
        r = pl.multiple_of(i * chunk, chunk)
        x = x_ref[pl.ds(r, chunk), :]
        ms = jnp.mean(x * x, axis=-1, keepdims=True)
        h_ref[pl.ds(r, chunk), :] = (x * lax.rsqrt(ms + EPS) * g_ref[...]).astype(BF16)
        return carry
    lax.fori_loop(0, rows // chunk, body, 0)


def _rms(y, g):
    ms = jnp.mean(y * y, axis=-1, keepdims=True)
    return y * lax.rsqrt(ms + EPS) * g


def _norm_matmul_kernel(x_ref, g_ref, w_ref, cs_ref, o_ref, h_ref, *, bm, act):
    @pl.when(pl.program_id(1) == 0)
    def _():
        _rms_rows_to_bf16(x_ref, g_ref, h_ref, bm)

    acc = jnp.dot(h_ref[...], w_ref[...], preferred_element_type=F32)
    if act == "sigmoid":
        acc = jax.nn.sigmoid(acc)
    elif act == "colscale":
        acc = acc * cs_ref[...]
    o_ref[...] = acc.astype(o_ref.dtype)


def norm_matmul(x, g, w, *, bm, bn, act="none", colscale=None):
    M, K = x.shape
    N = w.shape[1]
    assert M % bm == 0 and N % bn == 0
    if colscale is None:
        colscale = jnp.ones((N,), F32)
    return pl.pallas_call(
        functools.partial(_norm_matmul_kernel, bm=bm, act=act),
        grid=(M // bm, N // bn),
        in_specs=[
            pl.BlockSpec((bm, K), lambda i, j: (i, 0)),
            pl.BlockSpec((1, K), lambda i, j: (0, 0)),
            pl.BlockSpec((K, bn), lambda i, j: (0, j)),
            pl.BlockSpec((1, bn), lambda i, j: (0, j)),
        ],
        out_specs=pl.BlockSpec((bm, bn), lambda i, j: (i, j)),
        out_shape=jax.ShapeDtypeStruct((M, N), BF16),
        scratch_shapes=[pltpu.VMEM((bm, K), BF16)],
        compiler_params=_params(2),
        name="norm_matmul_" + act,
    )(x, g.reshape(1, K), w, colscale.reshape(1, N))


def _norm_gated_kernel(x_ref, g_ref, wa_ref, wb_ref, o_ref, h_ref, *, bm, mode):
    @pl.when(pl.program_id(1) == 0)
    def _():
        _rms_rows_to_bf16(x_ref, g_ref, h_ref, bm)

    h = h_ref[...]
    a = jnp.dot(h, wa_ref[...], preferred_element_type=F32)
    b = jnp.dot(h, wb_ref[...], preferred_element_type=F32)
    if mode == "glu":
        y = a * jax.nn.sigmoid(b)
    else:
        y = a * jax.nn.sigmoid(a) * b
    o_ref[...] = y.astype(o_ref.dtype)


def norm_gated_matmul(x, g, wa, wb, *, bm, bn, mode):
    M, K = x.shape
    N = wa.shape[1]
    assert M % bm == 0 and N % bn == 0
    return pl.pallas_call(
        functools.partial(_norm_gated_kernel, bm=bm, mode=mode),
        grid=(M // bm, N // bn),
        in_specs=[
            pl.BlockSpec((bm, K), lambda i, j: (i, 0)),
            pl.BlockSpec((1, K), lambda i, j: (0, 0)),
            pl.BlockSpec((K, bn), lambda i, j: (0, j)),
            pl.BlockSpec((K, bn), lambda i, j: (0, j)),
        ],
        out_specs=pl.BlockSpec((bm, bn), lambda i, j: (i, j)),
        out_shape=jax.ShapeDtypeStruct((M, N), BF16),
        scratch_shapes=[pltpu.VMEM((bm, K), BF16)],
        compiler_params=_params(2),
        name="norm_gated_" + mode,
    )(x, g.reshape(1, K), wa, wb)


def _matmul_norm_res_kernel(a_ref, w_ref, x_ref, g_ref, o_ref, acc_ref):
    k = pl.program_id(1)
    part = jnp.dot(a_ref[...], w_ref[...], preferred_element_type=F32)

    @pl.when(k == 0)
    def _():
        acc_ref[...] = part

    @pl.when(k > 0)
    def _():
        acc_ref[...] += part

    @pl.when(k == pl.num_programs(1) - 1)
    def _():
        o_ref[...] = x_ref[...] + _rms(acc_ref[...], g_ref[...])


def matmul_norm_residual(a, w, x, g, *, bm, bk):
    M, K = a.shape
    N = w.shape[1]
    assert M % bm == 0 and K % bk == 0
    return pl.pallas_call(
        _matmul_norm_res_kernel,
        grid=(M // bm, K // bk),
        in_specs=[
            pl.BlockSpec((bm, bk), lambda i, k: (i, k)),
            pl.BlockSpec((bk, N), lambda i, k: (k, 0)),
            pl.BlockSpec((bm, N), lambda i, k: (i, 0)),
            pl.BlockSpec((1, N), lambda i, k: (0, 0)),
        ],
        out_specs=pl.BlockSpec((bm, N), lambda i, k: (i, 0)),
        out_shape=jax.ShapeDtypeStruct((M, N), F32),
        scratch_shapes=[pltpu.VMEM((bm, N), F32)],
        compiler_params=_params(2),
        name="matmul_norm_residual",
    )(a, w, x, g.reshape(1, N))


LANES = 128
CONV_ROWS = 128
CONV_LN_ROWS = 16
CONV_LN_INTERLEAVE = 4
CONV_DATA_OFF = 16


def _conv_kernel(h_ref, wdw_ref, bdw_ref, lng_ref, lnb_ref, o_ref, hp_ref, cv_ref, *, S, C, KW):
    pad = KW // 2
    off = CONV_DATA_OFF
    n_slab = C // LANES
    R = CONV_ROWS

    for j in range(n_slab):
        cols = slice(j * LANES, (j + 1) * LANES)
        hp_ref[j, 0:off, :] = jnp.zeros((off, LANES), F32)
        hp_ref[j, off + S:off + S + off, :] = jnp.zeros((off, LANES), F32)

        def copy(i, carry, j=j, cols=cols):
            r = pl.multiple_of(i * 256, 256)
            hp_ref[j, pl.ds(off + r, 256), :] = h_ref[pl.ds(r, 256), cols].astype(F32)
            return carry
        lax.fori_loop(0, S // 256, copy, 0)

        def block(i, carry, j=j, cols=cols):
            for phase in range(2):
                t0 = i * (2 * R) + phase
                acc = jnp.broadcast_to(bdw_ref[:, cols], (R, LANES))
                for k in range(KW):
                    taps = hp_ref[j, pl.ds(t0 + (off - pad + k), R, stride=2), :]
                    acc = acc + taps * wdw_ref[k:k + 1, cols]
                cv_ref[j, pl.ds(t0, R, stride=2), :] = acc
            return carry
        lax.fori_loop(0, S // (2 * R), block, 0)

    def norm(i, carry):
        for u in range(CONV_LN_INTERLEAVE):
            start = (i * CONV_LN_INTERLEAVE + u) * CONV_LN_ROWS
            rows = pl.ds(pl.multiple_of(start, CONV_LN_ROWS), CONV_LN_ROWS)
            xs = [cv_ref[j, rows, :] for j in range(n_slab)]
            mu = jnp.sum(sum(xs), axis=-1, keepdims=True) * (1.0 / C)
            xs = [x - mu for x in xs]
            var = jnp.sum(sum(x * x for x in xs), axis=-1, keepdims=True) * (1.0 / C)
            inv = lax.rsqrt(var + EPS)
            for j in range(n_slab):
                cols = slice(j * LANES, (j + 1) * LANES)
                y = xs[j] * inv * lng_ref[:, cols] + lnb_ref[:, cols]
                o_ref[rows, cols] = (y * jax.nn.sigmoid(y)).astype(o_ref.dtype)
        return carry
    lax.fori_loop(0, S // (CONV_LN_ROWS * CONV_LN_INTERLEAVE), norm, 0)


def conv_branch(h, w_dw, b_dw, ln_g, ln_b):
    B, S, C = h.shape
    KW = w_dw.shape[0]
    assert KW // 2 < CONV_DATA_OFF and S % (2 * CONV_ROWS) == 0 and C % LANES == 0
    row = lambda v: v.reshape(1, C)
    vec = pl.BlockSpec((1, C), lambda b: (0, 0))
    return pl.pallas_call(
        functools.partial(_conv_kernel, S=S, C=C, KW=KW),
        grid=(B,),
        in_specs=[
            pl.BlockSpec((None, S, C), lambda b: (b, 0, 0)),
            pl.BlockSpec((KW, C), lambda b: (0, 0)),
            vec, vec, vec,
        ],
        out_specs=pl.BlockSpec((None, S, C), lambda b: (b, 0, 0)),
        out_shape=jax.ShapeDtypeStruct((B, S, C), BF16),
        scratch_shapes=[pltpu.VMEM((C // LANES, S + 2 * CONV_DATA_OFF, LANES), F32),
                        pltpu.VMEM((C // LANES, S, LANES), F32)],
        compiler_params=_params(1),
        name="conv_branch",
    )(h, w_dw, row(b_dw), row(ln_g), row(ln_b))


ATTN_Q_ROWS = 128


def _attn_kernel(q0, k0, v0, q1, k1, v1, q2, k2, v2, o_ref,
                 qf, kf, vf, qs, kp, vp, og0, og1, og2, lg0, lg1, lg2, bias_ref, s_buf, p_buf, *, S):
    T = ATTN_Q_ROWS
    HB = BAND_HALF
    W = T + 2 * HB
    D = HEAD_DIM
    n_blocks = S // T
    qkv = ((q0, k0, v0), (q1, k1, v1), (q2, k2, v2))
    outs = ((og0, lg0), (og1, lg1), (og2, lg2))
    slot_f = jnp.full((T, W), pl.program_id(1), jnp.int32).astype(F32)

    a_idx = lax.broadcasted_iota(jnp.int32, (T, W), 0)
    c_idx = lax.broadcasted_iota(jnp.int32, (T, W), 1)
    rel = jnp.abs(c_idx - HB - a_idx)
    c_row = lax.broadcasted_iota(jnp.int32, (1, W), 1)

    for g, (_, d) in enumerate(DIL_GROUPS):
        L = S // d
        P = L + 2 * HB
        nb = L // T
        q_ref, k_ref, v_ref = qkv[g]
        og, lg = outs[g]

        slope = jnp.exp2(-8.0 * (slot_f + (HEADS_PER_GROUP * g + 1.0)) / N_ATTN_HEADS)
        bias_ref[...] = jnp.where(rel <= HB, -(slope * (d * rel).astype(F32)), NEG_INF)

        def clear(i, carry):
            r0 = pl.multiple_of(i * T, T)
            kp[pl.ds(r0, T), :] = jnp.zeros((T, D), BF16)
            vp[pl.ds(r0, T), :] = jnp.zeros((T, D), BF16)
            return carry
        lax.fori_loop(0, d * P // T, clear, 0)

        if d == 1:
            def copy(i, carry, k_ref=k_ref, v_ref=v_ref):
                r0 = pl.multiple_of(i * 256, 256)
                kp[pl.ds(HB + r0, 256), :] = k_ref[pl.ds(r0, 256), :]
                vp[pl.ds(HB + r0, 256), :] = v_ref[pl.ds(r0, 256), :]
                return carry
            lax.fori_loop(0, S // 256, copy, 0)
            q_src = q_ref
        else:
            def widen(i, carry, q_ref=q_ref, k_ref=k_ref, v_ref=v_ref):
                r0 = pl.multiple_of(i * 256, 256)
                qf[pl.ds(r0, 256), :] = q_ref[pl.ds(r0, 256), :].astype(F32)
                kf[pl.ds(r0, 256), :] = k_ref[pl.ds(r0, 256), :].astype(F32)
                vf[pl.ds(r0, 256), :] = v_ref[pl.ds(r0, 256), :].astype(F32)
                return carry
            lax.fori_loop(0, S // 256, widen, 0)

            def gather(r, carry, d=d, L=L, P=P):
                for c in range(L // T):
                    src = pl.ds(r + d * c * T, T, stride=d)
                    qs[pl.ds(pl.multiple_of(r * L + c * T, T), T), :] = qf[src, :].astype(BF16)
                    dst = pl.ds(pl.multiple_of(r * P + (HB + c * T), HB), T)
                    kp[dst, :] = kf[src, :].astype(BF16)
                    vp[dst, :] = vf[src, :].astype(BF16)
                return carry
            lax.fori_loop(0, d, gather, 0, unroll=2)
            q_src = qs

        def geometry(n):
            r, i = divmod(n, nb)
            win = slice(r * P + i * T, r * P + i * T + W)
            dst = slice(n * T, (n + 1) * T) if d == 1 else pl.ds(r + d * i * T, T, stride=d)
            return i, win, dst

        for n in range(n_blocks):
            i, win, _ = geometry(n)
            s = lax.dot_general(q_src[n * T:(n + 1) * T, :], kp[win, :], (((1,), (1,)), ((), ())),
                                preferred_element_type=F32)
            s = s + bias_ref[...]
            kidx = c_row + (i * T - HB)
            if i == 0:
                s = s + jnp.where(kidx >= 0, 0.0, NEG_INF)
            if i == nb - 1:
                s = s + jnp.where(kidx < L, 0.0, NEG_INF)
            s_buf[n] = s
        for n in range(n_blocks):
            _, _, dst = geometry(n)
            s = s_buf[n]
            m = jnp.max(s, axis=-1, keepdims=True)
            p = jnp.exp(s - m)
            den = jnp.sum(p, axis=-1, keepdims=True)
            p_buf[n] = (p * (1.0 / den)).astype(BF16)
            lg[dst, :] = jnp.broadcast_to(m + jnp.log(den), (T, D))
        for n in range(n_blocks):
            _, win, dst = geometry(n)
            og[dst, :] = jnp.dot(p_buf[n], vp[win, :], preferred_element_type=F32)

    def mix(i, carry):
        r0 = pl.multiple_of(i * 256, 256)
        rows = pl.ds(r0, 256)
        l0, l1, l2 = lg0[rows, :], lg1[rows, :], lg2[rows, :]
        m = jnp.maximum(jnp.maximum(l0, l1), l2)
        w0, w1, w2 = jnp.exp(l0 - m), jnp.exp(l1 - m), jnp.exp(l2 - m)
        y = (w0 * og0[rows, :] + w1 * og1[rows, :] + w2 * og2[rows, :]) / (w0 + w1 + w2)
        o_ref[rows, :] = y.astype(o_ref.dtype)
        return carry
    lax.fori_loop(0, S // 256, mix, 0)


def dilated_mixture_attention(qkv):
    B, S, _ = qkv.shape
    D = HEAD_DIM
    T = ATTN_Q_ROWS
    assert all(S % (d * T) == 0 and w // (2 * d) == BAND_HALF for w, d in DIL_GROUPS)
    assert S % 256 == 0
    max_d = max(d for _, d in DIL_GROUPS)

    def head_spec(part, g):
        return pl.BlockSpec((None, S, D),
                            lambda b, j: (b, 0, part * N_ATTN_HEADS + HEADS_PER_GROUP * g + j))
    in_specs = [head_spec(part, g) for g in range(len(DIL_GROUPS)) for part in range(3)]
    seq_f32 = pltpu.VMEM((S, D), F32)
    return pl.pallas_call(
        functools.partial(_attn_kernel, S=S),
        grid=(B, HEADS_PER_GROUP),
        in_specs=in_specs,
        out_specs=pl.BlockSpec((None, S, D), lambda b, j: (b, 0, j)),
        out_shape=jax.ShapeDtypeStruct((B, S, HEADS_PER_GROUP * D), BF16),
        scratch_shapes=[
            seq_f32, seq_f32, seq_f32,
            pltpu.VMEM((S, D), BF16),
            pltpu.VMEM((S + 2 * BAND_HALF * max_d, D), BF16),
            pltpu.VMEM((S + 2 * BAND_HALF * max_d, D), BF16),
            seq_f32, seq_f32, seq_f32,
            seq_f32, seq_f32, seq_f32,
            pltpu.VMEM((T, T + 2 * BAND_HALF), F32),
            pltpu.VMEM((S // T, T, T + 2 * BAND_HALF), F32),
            pltpu.VMEM((S // T, T, T + 2 * BAND_HALF), BF16),
        ],
        compiler_params=_params(2),
        name="dilated_attention",
    )(*([qkv] * 9))


def _merge_kernel(c_ref, a_ref, ga_ref, gb_ref, x_ref, wc_ref, wa_ref, wo_ref, g_ref, o_ref):
    yc = jnp.dot(c_ref[...], wc_ref[...], preferred_element_type=F32)
    ya = jnp.dot(a_ref[...], wa_ref[...], preferred_element_type=F32)
    merged = ga_ref[...].astype(F32) * yc + gb_ref[...].astype(F32) * ya
    y = jnp.dot(merged.astype(BF16), wo_ref[...], preferred_element_type=F32)
    o_ref[...] = x_ref[...] + _rms(y, g_ref[...])


def merge_out(c, a, gates, x, wc, wa, wo, g, *, bm):
    M, D = x.shape
    Cc, Ca = c.shape[1], a.shape[1]
    assert M % bm == 0
    return pl.pallas_call(
        _merge_kernel,
        grid=(M // bm,),
        in_specs=[
            pl.BlockSpec((bm, Cc), lambda i: (i, 0)),
            pl.BlockSpec((bm, Ca), lambda i: (i, 0)),
            pl.BlockSpec((bm, D), lambda i: (i, 0)),
            pl.BlockSpec((bm, D), lambda i: (i, 1)),
            pl.BlockSpec((bm, D), lambda i: (i, 0)),
            _resident((Cc, D)), _resident((Ca, D)), _resident((D, D)), _resident((1, D)),
        ],
        out_specs=pl.BlockSpec((bm, D), lambda i: (i, 0)),
        out_shape=jax.ShapeDtypeStruct((M, D), F32),
        compiler_params=_params(1),
        name="merge_out",
    )(c, a, gates, gates, x, wc, wa, wo, g.reshape(1, D))


def _cross_kernel(x_ref, k_ref, v_ref, wq_ref, wo_ref, gpre_ref, gpost_ref, o_ref, h_ref, q_ref, ctx_ref,
                  *, bm, D):
    dh = D // X_HEADS
    _rms_rows_to_bf16(x_ref, gpre_ref, h_ref, bm)
    q = jnp.dot(h_ref[...], wq_ref[...], preferred_element_type=F32)
    q_ref[...] = (q * (dh ** -0.5)).astype(BF16)
    for hd in range(X_HEADS):
        cols = slice(hd * dh, (hd + 1) * dh)
        s = lax.dot_general(q_ref[:, cols], k_ref[:, cols], (((1,), (1,)), ((), ())),
                            preferred_element_type=F32)
        m = jnp.max(s, axis=-1, keepdims=True)
        p = jnp.exp(s - m)
        den = jnp.sum(p, axis=-1, keepdims=True)
        ctx = jnp.dot(p.astype(BF16), v_ref[:, cols], preferred_element_type=F32) / den
        ctx_ref[:, cols] = ctx.astype(BF16)
    y = jnp.dot(ctx_ref[...], wo_ref[...], preferred_element_type=F32)
    o_ref[...] = x_ref[...] + _rms(y, gpost_ref[...])


def cross_attention_sublayer(x, kmem, vmem, wq, wo, g_pre, g_post, *, bm):
    B, S, D = x.shape
    Nm = kmem.shape[1]
    assert S % bm == 0 and bm % 64 == 0
    tile = pl.BlockSpec((None, bm, D), lambda b, i: (b, i, 0))
    mem_spec = pl.BlockSpec((None, Nm, D), lambda b, i: (b, 0, 0))
    return pl.pallas_call(
        functools.partial(_cross_kernel, bm=bm, D=D),
        grid=(B, S // bm),
        in_specs=[tile, mem_spec, mem_spec, _resident((D, D)), _resident((D, D)),
                  _resident((1, D)), _resident((1, D))],
        out_specs=tile,
        out_shape=jax.ShapeDtypeStruct((B, S, D), F32),
        scratch_shapes=[pltpu.VMEM((bm, D), BF16), pltpu.VMEM((bm, D), BF16), pltpu.VMEM((bm, D), BF16)],
        compiler_params=_params(2),
        name="cross_attention",
    )(x, kmem, vmem, wq, wo, g_pre.reshape(1, D), g_post.reshape(1, D))


def kernel(x, mem, w_in, w_dw, b_dw, conv_ln_g, conv_ln_b, w_conv_out, w_attn_proj, w_o, g_mix_pre, g_mix_post, g_mem, w_cq, w_ck, w_cv, w_co, g_x_pre, g_x_post, w_ffn_gate, w_ffn_up, w_ffn_down, g_ffn_pre, g_ffn_post):
    B, S, D = x.shape
    depth = w_in.shape[0]
    Cc = w_dw.shape[-1]
    Wq = N_ATTN_HEADS * HEAD_DIM
    Nm = mem.shape[1]
    c0 = 2 * Cc
    c3 = c0 + 3 * Wq
    M = B * S

    q_scale = jnp.concatenate([jnp.full((Wq,), HEAD_DIM ** -0.5, F32), jnp.ones((2 * Wq,), F32)])
    mem2 = mem.reshape(B * Nm, D)
    x2 = x.reshape(M, D)
    bf = lambda w: w.astype(BF16)

    for l in range(depth):
        glu = norm_gated_matmul(x2, g_mix_pre[l], bf(w_in[l, :, :Cc]), bf(w_in[l, :, Cc:c0]),
                                bm=1024, bn=512, mode="glu")
        qkv = norm_matmul(x2, g_mix_pre[l], bf(w_in[l, :, c0:c3]), bm=1024, bn=1536,
                          act="colscale", colscale=q_scale)
        gates = norm_matmul(x2, g_mix_pre[l], bf(w_in[l, :, c3:]), bm=1024, bn=1024, act="sigmoid")
        conv = conv_branch(glu.reshape(B, S, Cc), w_dw[l], b_dw[l], conv_ln_g[l], conv_ln_b[l])
        attn = dilated_mixture_attention(qkv.reshape(B, S, 3 * Wq))
        x2 = merge_out(conv.reshape(M, Cc), attn.reshape(M, -1), gates, x2,
                       bf(w_conv_out[l]), bf(w_attn_proj[l]), bf(w_o[l]), g_mix_post[l], bm=256)
        kmem = norm_matmul(mem2, g_mem[l], bf(w_ck[l]), bm=1024, bn=1024)
        vmem = norm_matmul(mem2, g_mem[l], bf(w_cv[l]), bm=1024, bn=1024)
        x2 = cross_attention_sublayer(x2.reshape(B, S, D), kmem.reshape(B, Nm, D), vmem.reshape(B, Nm, D),
                                      bf(w_cq[l]), bf(w_co[l]), g_x_pre[l], g_x_post[l], bm=256).reshape(M, D)
        act = norm_gated_matmul(x2, g_ffn_pre[l], bf(w_ffn_gate[l]), bf(w_ffn_up[l]),
                                bm=1024, bn=512, mode="swiglu")
        x2 = matmul_norm_residual(act, bf(w_ffn_down[l]), x2, g_ffn_post[l], bm=512, bk=1408)
    return x2.reshape(B, S, D)
```

```python
import functools

import jax
import jax.numpy as jnp
from jax import lax
from jax.experimental import pallas as pl
from jax.experimental.pallas import tpu as pltpu

F32 = jnp.float32
BF16 = jnp.bfloat16

EPS = 1e-6
NEG_INF = -1e30

HEAD_DIM = 128
DIL_GROUPS = ((128, 1), (512, 4), (2048, 16))
HEADS_PER_GROUP = 4
N_ATTN_HEADS = HEADS_PER_GROUP * len(DIL_GROUPS)
BAND_HALF = 64
X_HEADS = 4

V7X_VMEM_LIMIT_BYTES = 56 * 1024 * 1024
LANES = 128

PROJ_COLS = 512
PROJ_ROWS_SINGLE = 2048
PROJ_ROWS_GATED = 1024
ROW_TILE = 256
NORM_ROWS = 1024


def _params(n_grid_dims):
    return pltpu.CompilerParams(
        dimension_semantics=("arbitrary",) * n_grid_dims,
        vmem_limit_bytes=V7X_VMEM_LIMIT_BYTES)


def _resident(shape):
    return pl.BlockSpec(shape, lambda *_: (0,) * len(shape), pipeline_mode=pl.Buffered(1))


def _rms(y, g):
    ms = jnp.mean(y * y, axis=-1, keepdims=True)
    return y * lax.rsqrt(ms + EPS) * g


def _norm_kernel(x_ref, g_ref, o_ref, *, bm, chunk=64):
    def body(i, carry):
        r = pl.multiple_of(i * chunk, chunk)
        o_ref[pl.ds(r, chunk), :] = _rms(x_ref[pl.ds(r, chunk), :], g_ref[...]).astype(BF16)
        return carry
    lax.fori_loop(0, bm // chunk, body, 0)


def rms_norm_bf16(x, g):
    M, K = x.shape
    bm = NORM_ROWS
    assert M % bm == 0
    return pl.pallas_call(
        functools.partial(_norm_kernel, bm=bm),
        grid=(M // bm,),
        in_specs=[pl.BlockSpec((bm, K), lambda i: (i, 0)), pl.BlockSpec((1, K), lambda i: (0, 0))],
        out_specs=pl.BlockSpec((bm, K), lambda i: (i, 0)),
        out_shape=jax.ShapeDtypeStruct((M, K), BF16),
        compiler_params=_params(1),
        name="rms_norm",
    )(x, g.reshape(1, K))


def _weight_spec(w_stack, layer, col0):
    K = w_stack.shape[1]
    assert col0 % PROJ_COLS == 0
    return pl.BlockSpec((None, K, PROJ_COLS), lambda j, i: (layer, 0, col0 // PROJ_COLS + j))


def _proj_kernel(*refs, act):
    if act == "colscale":
        h_ref, w_ref, cs_ref, o_ref, wb_ref = refs
    else:
        h_ref, w_ref, o_ref, wb_ref = refs

    @pl.when(pl.program_id(1) == 0)
    def _():
        wb_ref[...] = w_ref[...].astype(BF16)

    acc = jnp.dot(h_ref[...], wb_ref[...], preferred_element_type=F32)
    if act == "sigmoid":
        acc = jax.nn.sigmoid(acc)
    elif act == "colscale":
        acc = acc * cs_ref[...]
    o_ref[...] = acc.astype(o_ref.dtype)


def project(h, w_stack, layer, col0, n_cols, *, act="none", colscale=None):
    M, K = h.shape
    bm, bn = min(PROJ_ROWS_SINGLE, M), PROJ_COLS
    assert M % bm == 0 and n_cols % bn == 0 and (act == "colscale") == (colscale is not None)
    in_specs = [pl.BlockSpec((bm, K), lambda j, i: (i, 0)), _weight_spec(w_stack, layer, col0)]
    args = [h, w_stack]
    if colscale is not None:
        in_specs.append(pl.BlockSpec((1, bn), lambda j, i: (0, j)))
        args.append(colscale.reshape(1, n_cols))
    return pl.pallas_call(
        functools.partial(_proj_kernel, act=act),
        grid=(n_cols // bn, M // bm),
        in_specs=in_specs,
        out_specs=pl.BlockSpec((bm, bn), lambda j, i: (i, j)),
        out_shape=jax.ShapeDtypeStruct((M, n_cols), BF16),
        scratch_shapes=[pltpu.VMEM((K, bn), BF16)],
        compiler_params=_params(2),
        name="project_" + act,
    )(*args)


def _proj_gated_kernel(h_ref, wa_ref, wb_ref, o_ref, wa_bf, wb_bf, *, mode):
    @pl.when(pl.program_id(1) == 0)
    def _():
        wa_bf[...] = wa_ref[...].astype(BF16)
        wb_bf[...] = wb_ref[...].astype(BF16)

    h = h_ref[...]
    a = jnp.dot(h, wa_bf[...], preferred_element_type=F32)
    b = jnp.dot(h, wb_bf[...], preferred_element_type=F32)
    if mode == "glu":
        y = a * jax.nn.sigmoid(b)
    else:
        y = a * jax.nn.sigmoid(a) * b
    o_ref[...] = y.astype(o_ref.dtype)


def project_gated(h, wa, wb, n_cols, *, mode):
    M, K = h.shape
    bm, bn = PROJ_ROWS_GATED, PROJ_COLS
    assert M % bm == 0 and n_cols % bn == 0
    return pl.pallas_call(
        functools.partial(_proj_gated_kernel, mode=mode),
        grid=(n_cols // bn, M // bm),
        in_specs=[pl.BlockSpec((bm, K), lambda j, i: (i, 0)), _weight_spec(*wa), _weight_spec(*wb)],
        out_specs=pl.BlockSpec((bm, bn), lambda j, i: (i, j)),
        out_shape=jax.ShapeDtypeStruct((M, n_cols), BF16),
        scratch_shapes=[pltpu.VMEM((K, bn), BF16), pltpu.VMEM((K, bn), BF16)],
        compiler_params=_params(2),
        name="project_" + mode,
    )(h, wa[0], wb[0])


def _down_kernel(*refs, emit_next):
    if emit_next:
        a_ref, w_ref, x_ref, g_ref, gn_ref, o_ref, hn_ref = refs
    else:
        a_ref, w_ref, x_ref, g_ref, o_ref = refs
    y = jnp.dot(a_ref[...], w_ref[...], preferred_element_type=F32)
    xn = x_ref[...] + _rms(y, g_ref[...])
    o_ref[...] = xn
    if emit_next:
        hn_ref[...] = _rms(xn, gn_ref[...]).astype(BF16)


def down_residual(a, w, x, g, g_next):
    M, K = a.shape
    N = w.shape[1]
    bm = ROW_TILE
    emit_next = g_next is not None
    assert M % bm == 0
    tile = pl.BlockSpec((bm, N), lambda i: (i, 0))
    in_specs = [pl.BlockSpec((bm, K), lambda i: (i, 0)), _resident((K, N)), tile, _resident((1, N))]
    args = [a, w, x, g.reshape(1, N)]
    out_specs, out_shape = tile, jax.ShapeDtypeStruct((M, N), F32)
    if emit_next:
        in_specs.append(_resident((1, N)))
        args.append(g_next.reshape(1, N))
        out_specs, out_shape = (tile, tile), (out_shape, jax.ShapeDtypeStruct((M, N), BF16))
    res = pl.pallas_call(
        functools.partial(_down_kernel, emit_next=emit_next),
        grid=(M // bm,),
        in_specs=in_specs,
        out_specs=out_specs,
        out_shape=out_shape,
        compiler_params=_params(1),
        name="down_residual",
    )(*args)
    return res if emit_next else (res, None)


CONV_ROWS = 128
CONV_LN_ROWS = 16
CONV_LN_INTERLEAVE = 4
CONV_DATA_OFF = 16


def _conv_kernel(h_ref, wdw_ref, bdw_ref, lng_ref, lnb_ref, o_ref, hp_ref, cv_ref, *, S, C, KW):
    pad = KW // 2
    off = CONV_DATA_OFF
    n_slab = C // LANES
    R = CONV_ROWS

    for j in range(n_slab):
        cols = slice(j * LANES, (j + 1) * LANES)
        hp_ref[j, 0:off, :] = jnp.zeros((off, LANES), F32)
        hp_ref[j, off + S:off + S + off, :] = jnp.zeros((off, LANES), F32)

        def copy(i, carry, j=j, cols=cols):
            r = pl.multiple_of(i * 256, 256)
            hp_ref[j, pl.ds(off + r, 256), :] = h_ref[pl.ds(r, 256), cols].astype(F32)
            return carry
        lax.fori_loop(0, S // 256, copy, 0)

        def block(i, carry, j=j, cols=cols):
            for phase in range(2):
                t0 = i * (2 * R) + phase
                acc = jnp.broadcast_to(bdw_ref[:, cols], (R, LANES))
                for k in range(KW):
                    taps = hp_ref[j, pl.ds(t0 + (off - pad + k), R, stride=2), :]
                    acc = acc + taps * wdw_ref[k:k + 1, cols]
                cv_ref[j, pl.ds(t0, R, stride=2), :] = acc
            return carry
        lax.fori_loop(0, S // (2 * R), block, 0)

    def norm(i, carry):
        for u in range(CONV_LN_INTERLEAVE):
            start = (i * CONV_LN_INTERLEAVE + u) * CONV_LN_ROWS
            rows = pl.ds(pl.multiple_of(start, CONV_LN_ROWS), CONV_LN_ROWS)
            xs = [cv_ref[j, rows, :] for j in range(n_slab)]
            mu = jnp.sum(sum(xs), axis=-1, keepdims=True) * (1.0 / C)
            xs = [x - mu for x in xs]
            var = jnp.sum(sum(x * x for x in xs), axis=-1, keepdims=True) * (1.0 / C)
            inv = lax.rsqrt(var + EPS)
            for j in range(n_slab):
                cols = slice(j * LANES, (j + 1) * LANES)
                y = xs[j] * inv * lng_ref[:, cols] + lnb_ref[:, cols]
                o_ref[rows, cols] = (y * jax.nn.sigmoid(y)).astype(o_ref.dtype)
        return carry
    lax.fori_loop(0, S // (CONV_LN_ROWS * CONV_LN_INTERLEAVE), norm, 0)


def conv_branch(h, w_dw, b_dw, ln_g, ln_b):
    B, S, C = h.shape
    KW = w_dw.shape[0]
    assert KW // 2 < CONV_DATA_OFF and S % (2 * CONV_ROWS) == 0 and C % LANES == 0
    row = lambda v: v.reshape(1, C)
    vec = pl.BlockSpec((1, C), lambda b: (0, 0))
    return pl.pallas_call(
        functools.partial(_conv_kernel, S=S, C=C, KW=KW),
        grid=(B,),
        in_specs=[
            pl.BlockSpec((None, S, C), lambda b: (b, 0, 0)),
            pl.BlockSpec((KW, C), lambda b: (0, 0)),
            vec, vec, vec,
        ],
        out_specs=pl.BlockSpec((None, S, C), lambda b: (b, 0, 0)),
        out_shape=jax.ShapeDtypeStruct((B, S, C), BF16),
        scratch_shapes=[pltpu.VMEM((C // LANES, S + 2 * CONV_DATA_OFF, LANES), F32),
                        pltpu.VMEM((C // LANES, S, LANES), F32)],
        compiler_params=_params(1),
        name="conv_branch",
    )(h, w_dw, row(b_dw), row(ln_g), row(ln_b))


ATTN_Q_ROWS = 128


def _attn_kernel(q0, k0, v0, q1, k1, v1, q2, k2, v2, o_ref,
                 qf, kf, vf, qs, kp, vp, og0, og1, og2, lg0, lg1, lg2, bias_ref, s_buf, p_buf, *, S):
    T = ATTN_Q_ROWS
    HB = BAND_HALF
    W = T + 2 * HB
    D = HEAD_DIM
    n_blocks = S // T
    qkv = ((q0, k0, v0), (q1, k1, v1), (q2, k2, v2))
    outs = ((og0, lg0), (og1, lg1), (og2, lg2))
    slot_f = jnp.full((T, W), pl.program_id(1), jnp.int32).astype(F32)

    a_idx = lax.broadcasted_iota(jnp.int32, (T, W), 0)
    c_idx = lax.broadcasted_iota(jnp.int32, (T, W), 1)
    rel = jnp.abs(c_idx - HB - a_idx)
    c_row = lax.broadcasted_iota(jnp.int32, (1, W), 1)

    for g, (_, d) in enumerate(DIL_GROUPS):
        L = S // d
        P = L + 2 * HB
        nb = L // T
        q_ref, k_ref, v_ref = qkv[g]
        og, lg = outs[g]

        slope = jnp.exp2(-8.0 * (slot_f + (HEADS_PER_GROUP * g + 1.0)) / N_ATTN_HEADS)
        bias_ref[...] = jnp.where(rel <= HB, -(slope * (d * rel).astype(F32)), NEG_INF)

        def clear(i, carry):
            r0 = pl.multiple_of(i * T, T)
            kp[pl.ds(r0, T), :] = jnp.zeros((T, D), BF16)
            vp[pl.ds(r0, T), :] = jnp.zeros((T, D), BF16)
            return carry
        lax.fori_loop(0, d * P // T, clear, 0)

        if d == 1:
            def copy(i, carry, k_ref=k_ref, v_ref=v_ref):
                r0 = pl.multiple_of(i * 256, 256)
                kp[pl.ds(HB + r0, 256), :] = k_ref[pl.ds(r0, 256), :]
                vp[pl.ds(HB + r0, 256), :] = v_ref[pl.ds(r0, 256), :]
                return carry
            lax.fori_loop(0, S // 256, copy, 0)
            q_src = q_ref
        else:
            def widen(i, carry, q_ref=q_ref, k_ref=k_ref, v_ref=v_ref):
                r0 = pl.multiple_of(i * 256, 256)
                qf[pl.ds(r0, 256), :] = q_ref[pl.ds(r0, 256), :].astype(F32)
                kf[pl.ds(r0, 256), :] = k_ref[pl.ds(r0, 256), :].astype(F32)
                vf[pl.ds(r0, 256), :] = v_ref[pl.ds(r0, 256), :].astype(F32)
                return carry
            lax.fori_loop(0, S // 256, widen, 0)

            def gather(r, carry, d=d, L=L, P=P):
                for c in range(L // T):
                    src = pl.ds(r + d * c * T, T, stride=d)
                    qs[pl.ds(pl.multiple_of(r * L + c * T, T), T), :] = qf[src, :].astype(BF16)
                    dst = pl.ds(pl.multiple_of(r * P + (HB + c * T), HB), T)
                    kp[dst, :] = kf[src, :].astype(BF16)
                    vp[dst, :] = vf[src, :].astype(BF16)
                return carry
            lax.fori_loop(0, d, gather, 0, unroll=2)
            q_src = qs

        def geometry(n):
            r, i = divmod(n, nb)
            win = slice(r * P + i * T, r * P + i * T + W)
            dst = slice(n * T, (n + 1) * T) if d == 1 else pl.ds(r + d * i * T, T, stride=d)
            return i, win, dst

        for n in range(n_blocks):
            i, win, _ = geometry(n)
            s = lax.dot_general(q_src[n * T:(n + 1) * T, :], kp[win, :], (((1,), (1,)), ((), ())),
                                preferred_element_type=F32)
            s = s + bias_ref[...]
            kidx = c_row + (i * T - HB)
            if i == 0:
                s = s + jnp.where(kidx >= 0, 0.0, NEG_INF)
            if i == nb - 1:
                s = s + jnp.where(kidx < L, 0.0, NEG_INF)
            s_buf[n] = s
        for n in range(n_blocks):
            _, _, dst = geometry(n)
            s = s_buf[n]
            m = jnp.max(s, axis=-1, keepdims=True)
            p = jnp.exp(s - m)
            den = jnp.sum(p, axis=-1, keepdims=True)
            p_buf[n] = (p * (1.0 / den)).astype(BF16)
            lg[dst, :] = jnp.broadcast_to(m + jnp.log(den), (T, D))
        for n in range(n_blocks):
            _, win, dst = geometry(n)
            og[dst, :] = jnp.dot(p_buf[n], vp[win, :], preferred_element_type=F32)

    def mix(i, carry):
        r0 = pl.multiple_of(i * 256, 256)
        rows = pl.ds(r0, 256)
        l0, l1, l2 = lg0[rows, :], lg1[rows, :], lg2[rows, :]
        m = jnp.maximum(jnp.maximum(l0, l1), l2)
        w0, w1, w2 = jnp.exp(l0 - m), jnp.exp(l1 - m), jnp.exp(l2 - m)
        y = (w0 * og0[rows, :] + w1 * og1[rows, :] + w2 * og2[rows, :]) / (w0 + w1 + w2)
        o_ref[rows, :] = y.astype(o_ref.dtype)
        return carry
    lax.fori_loop(0, S // 256, mix, 0)


def dilated_mixture_attention(qkv):
    B, S, _ = qkv.shape
    D = HEAD_DIM
    T = ATTN_Q_ROWS
    assert all(S % (d * T) == 0 and w // (2 * d) == BAND_HALF for w, d in DIL_GROUPS)
    assert S % 256 == 0
    max_d = max(d for _, d in DIL_GROUPS)

    def head_spec(part, g):
        return pl.BlockSpec((None, S, D),
                            lambda b, j: (b, 0, part * N_ATTN_HEADS + HEADS_PER_GROUP * g + j))
    in_specs = [head_spec(part, g) for g in range(len(DIL_GROUPS)) for part in range(3)]
    seq_f32 = pltpu.VMEM((S, D), F32)
    return pl.pallas_call(
        functools.partial(_attn_kernel, S=S),
        grid=(B, HEADS_PER_GROUP),
        in_specs=in_specs,
        out_specs=pl.BlockSpec((None, S, D), lambda b, j: (b, 0, j)),
        out_shape=jax.ShapeDtypeStruct((B, S, HEADS_PER_GROUP * D), BF16),
        scratch_shapes=[
            seq_f32, seq_f32, seq_f32,
            pltpu.VMEM((S, D), BF16),
            pltpu.VMEM((S + 2 * BAND_HALF * max_d, D), BF16),
            pltpu.VMEM((S + 2 * BAND_HALF * max_d, D), BF16),
            seq_f32, seq_f32, seq_f32,
            seq_f32, seq_f32, seq_f32,
            pltpu.VMEM((T, T + 2 * BAND_HALF), F32),
            pltpu.VMEM((S // T, T, T + 2 * BAND_HALF), F32),
            pltpu.VMEM((S // T, T, T + 2 * BAND_HALF), BF16),
        ],
        compiler_params=_params(2),
        name="dilated_attention",
    )(*([qkv] * 9))


def _merge_kernel(c_ref, a_ref, ga_ref, gb_ref, x_ref, wc_ref, wa_ref, wo_ref, g_ref, o_ref):
    yc = jnp.dot(c_ref[...], wc_ref[...], preferred_element_type=F32)
    ya = jnp.dot(a_ref[...], wa_ref[...], preferred_element_type=F32)
    merged = ga_ref[...].astype(F32) * yc + gb_ref[...].astype(F32) * ya
    y = jnp.dot(merged.astype(BF16), wo_ref[...], preferred_element_type=F32)
    o_ref[...] = x_ref[...] + _rms(y, g_ref[...])


def merge_out(c, a, gates, x, wc, wa, wo, g):
    M, D = x.shape
    Cc, Ca = c.shape[1], a.shape[1]
    bm = ROW_TILE
    assert M % bm == 0
    return pl.pallas_call(
        _merge_kernel,
        grid=(M // bm,),
        in_specs=[
            pl.BlockSpec((bm, Cc), lambda i: (i, 0)),
            pl.BlockSpec((bm, Ca), lambda i: (i, 0)),
            pl.BlockSpec((bm, D), lambda i: (i, 0)),
            pl.BlockSpec((bm, D), lambda i: (i, 1)),
            pl.BlockSpec((bm, D), lambda i: (i, 0)),
            _resident((Cc, D)), _resident((Ca, D)), _resident((D, D)), _resident((1, D)),
        ],
        out_specs=pl.BlockSpec((bm, D), lambda i: (i, 0)),
        out_shape=jax.ShapeDtypeStruct((M, D), F32),
        compiler_params=_params(1),
        name="merge_out",
    )(c, a, gates, gates, x, wc, wa, wo, g.reshape(1, D))


def _cross_kernel(x_ref, k_ref, v_ref, wq_ref, wo_ref, gpre_ref, gpost_ref, gn_ref, o_ref, hn_ref, *, D):
    dh = D // X_HEADS

    heads = [slice(hd * dh, (hd + 1) * dh) for hd in range(X_HEADS)]
    h = _rms(x_ref[...], gpre_ref[...]).astype(BF16)
    q = (jnp.dot(h, wq_ref[...], preferred_element_type=F32) * (dh ** -0.5)).astype(BF16)
    scores = [lax.dot_general(q[:, cols], k_ref[:, cols], (((1,), (1,)), ((), ())),
                              preferred_element_type=F32) for cols in heads]
    probs = []
    for s in scores:
        p = jnp.exp(s - jnp.max(s, axis=-1, keepdims=True))
        probs.append((p * (1.0 / jnp.sum(p, axis=-1, keepdims=True))).astype(BF16))
    ctx = [jnp.dot(p, v_ref[:, cols], preferred_element_type=F32).astype(BF16)
           for p, cols in zip(probs, heads)]
    y = jnp.dot(jnp.concatenate(ctx, axis=-1), wo_ref[...], preferred_element_type=F32)
    xn = x_ref[...] + _rms(y, gpost_ref[...])
    o_ref[...] = xn
    hn_ref[...] = _rms(xn, gn_ref[...]).astype(BF16)


def cross_attention_sublayer(x, kmem, vmem, wq, wo, g_pre, g_post, g_next):
    B, S, D = x.shape
    Nm = kmem.shape[1]
    bm = ROW_TILE
    assert S % bm == 0
    tile = pl.BlockSpec((None, bm, D), lambda b, i: (b, i, 0))
    mem_spec = pl.BlockSpec((None, Nm, D), lambda b, i: (b, 0, 0))
    vec = lambda g: g.reshape(1, D)
    return pl.pallas_call(
        functools.partial(_cross_kernel, D=D),
        grid=(B, S // bm),
        in_specs=[tile, mem_spec, mem_spec, _resident((D, D)), _resident((D, D)),
                  _resident((1, D)), _resident((1, D)), _resident((1, D))],
        out_specs=(tile, tile),
        out_shape=(jax.ShapeDtypeStruct((B, S, D), F32), jax.ShapeDtypeStruct((B, S, D), BF16)),
        compiler_params=_params(2),
        name="cross_attention",
    )(x, kmem, vmem, wq, wo, vec(g_pre), vec(g_post), vec(g_next))


def kernel(x, mem, w_in, w_dw, b_dw, conv_ln_g, conv_ln_b, w_conv_out, w_attn_proj, w_o, g_mix_pre, g_mix_post, g_mem, w_cq, w_ck, w_cv, w_co, g_x_pre, g_x_post, w_ffn_gate, w_ffn_up, w_ffn_down, g_ffn_pre, g_ffn_post):
    B, S, D = x.shape
    depth = w_in.shape[0]
    Cc = w_dw.shape[-1]
    Wq = N_ATTN_HEADS * HEAD_DIM
    Nm = mem.shape[1]
    d_ff = w_ffn_gate.shape[-1]
    c0 = 2 * Cc
    c3 = c0 + 3 * Wq
    M = B * S

    q_scale = jnp.concatenate([jnp.full((Wq,), HEAD_DIM ** -0.5, F32), jnp.ones((2 * Wq,), F32)])
    mem2 = mem.reshape(B * Nm, D)
    x2 = x.reshape(M, D)
    bf = lambda w: w.astype(BF16)

    h = rms_norm_bf16(x2, g_mix_pre[0])
    for l in range(depth):
        glu = project_gated(h, (w_in, l, 0), (w_in, l, Cc), Cc, mode="glu")
        qkv = project(h, w_in, l, c0, 3 * Wq, act="colscale", colscale=q_scale)
        gates = project(h, w_in, l, c3, 2 * D, act="sigmoid")
        conv = conv_branch(glu.reshape(B, S, Cc), w_dw[l], b_dw[l], conv_ln_g[l], conv_ln_b[l])
        attn = dilated_mixture_attention(qkv.reshape(B, S, 3 * Wq))
        x2 = merge_out(conv.reshape(M, Cc), attn.reshape(M, -1), gates, x2,
                       bf(w_conv_out[l]), bf(w_attn_proj[l]), bf(w_o[l]), g_mix_post[l])
        memn = rms_norm_bf16(mem2, g_mem[l])
        kmem = project(memn, w_ck, l, 0, D)
        vmem = project(memn, w_cv, l, 0, D)
        x3, h3 = cross_attention_sublayer(x2.reshape(B, S, D), kmem.reshape(B, Nm, D), vmem.reshape(B, Nm, D),
                                          bf(w_cq[l]), bf(w_co[l]), g_x_pre[l], g_x_post[l], g_ffn_pre[l])
        x2, h = x3.reshape(M, D), h3.reshape(M, D)
        act = project_gated(h, (w_ffn_gate, l, 0), (w_ffn_up, l, 0), d_ff, mode="swiglu")
        x2, h = down_residual(act, bf(w_ffn_down[l]), x2, g_ffn_post[l],
                              g_mix_pre[l + 1] if l + 1 < depth else None)
    return x2.reshape(B, S, D)
```

```python
import functools

import jax
import jax.numpy as jnp
from jax import lax
from jax.experimental import pallas as pl
from jax.experimental.pallas import tpu as pltpu

F32 = jnp.float32
BF16 = jnp.bfloat16

EPS = 1e-6
NEG_INF = -1e30

HEAD_DIM = 128
DIL_GROUPS = ((128, 1), (512, 4), (2048, 16))
HEADS_PER_GROUP = 4
N_ATTN_HEADS = HEADS_PER_GROUP * len(DIL_GROUPS)
BAND_HALF = 64
X_HEADS = 4

V7X_VMEM_LIMIT_BYTES = 56 * 1024 * 1024
LANES = 128

PROJ_TILE_GATED = (1024, 512)
PROJ_TILE_QKV = (2048, 768)
PROJ_TILE_WIDE = (1024, 1024)
ROW_TILE = 256
NORM_ROWS = 1024


def _params(n_grid_dims):
    return pltpu.CompilerParams(
        dimension_semantics=("arbitrary",) * n_grid_dims,
        vmem_limit_bytes=V7X_VMEM_LIMIT_BYTES)


def _resident(shape):
    return pl.BlockSpec(shape, lambda *_: (0,) * len(shape), pipeline_mode=pl.Buffered(1))


def _rms(y, g):
    ms = jnp.mean(y * y, axis=-1, keepdims=True)
    return y * lax.rsqrt(ms + EPS) * g


def _norm_kernel(x_ref, g_ref, o_ref, *, bm, chunk=64):
    def body(i, carry):
        r = pl.multiple_of(i * chunk, chunk)
        o_ref[pl.ds(r, chunk), :] = _rms(x_ref[pl.ds(r, chunk), :], g_ref[...]).astype(BF16)
        return carry
    lax.fori_loop(0, bm // chunk, body, 0)


def rms_norm_bf16(x, g):
    M, K = x.shape
    bm = NORM_ROWS
    assert M % bm == 0
    return pl.pallas_call(
        functools.partial(_norm_kernel, bm=bm),
        grid=(M // bm,),
        in_specs=[pl.BlockSpec((bm, K), lambda i: (i, 0)), pl.BlockSpec((1, K), lambda i: (0, 0))],
        out_specs=pl.BlockSpec((bm, K), lambda i: (i, 0)),
        out_shape=jax.ShapeDtypeStruct((M, K), BF16),
        compiler_params=_params(1),
        name="rms_norm",
    )(x, g.reshape(1, K))


def _weight_spec(w_stack, layer, col0, bn):
    K = w_stack.shape[1]
    assert col0 % LANES == 0 and bn % LANES == 0
    return pl.BlockSpec((pl.Element(1), pl.Element(K), pl.Element(bn)),
                        lambda j, i: (layer, 0, pl.multiple_of(col0 + j * bn, LANES)))


def _proj_kernel(*refs, act):
    if act == "colscale":
        h_ref, w_ref, cs_ref, o_ref, wb_ref = refs
    else:
        h_ref, w_ref, o_ref, wb_ref = refs

    @pl.when(pl.program_id(1) == 0)
    def _():
        wb_ref[...] = w_ref[0].astype(BF16)

    acc = jnp.dot(h_ref[...], wb_ref[...], preferred_element_type=F32)
    if act == "sigmoid":
        acc = jax.nn.sigmoid(acc)
    elif act == "colscale":
        acc = acc * cs_ref[...]
    o_ref[...] = acc.astype(o_ref.dtype)


def project(h, w_stack, layer, col0, n_cols, tile, *, act="none", colscale=None):
    M, K = h.shape
    bm, bn = tile
    assert M % bm == 0 and n_cols % bn == 0 and (act == "colscale") == (colscale is not None)
    in_specs = [pl.BlockSpec((bm, K), lambda j, i: (i, 0)), _weight_spec(w_stack, layer, col0, bn)]
    args = [h, w_stack]
    if colscale is not None:
        in_specs.append(pl.BlockSpec((1, bn), lambda j, i: (0, j)))
        args.append(colscale.reshape(1, n_cols))
    return pl.pallas_call(
        functools.partial(_proj_kernel, act=act),
        grid=(n_cols // bn, M // bm),
        in_specs=in_specs,
        out_specs=pl.BlockSpec((bm, bn), lambda j, i: (i, j)),
        out_shape=jax.ShapeDtypeStruct((M, n_cols), BF16),
        scratch_shapes=[pltpu.VMEM((K, bn), BF16)],
        compiler_params=_params(2),
        name="project_" + act,
    )(*args)


def _proj_gated_kernel(h_ref, wa_ref, wb_ref, o_ref, wa_bf, wb_bf, *, mode):
    @pl.when(pl.program_id(1) == 0)
    def _():
        wa_bf[...] = wa_ref[0].astype(BF16)
        wb_bf[...] = wb_ref[0].astype(BF16)

    h = h_ref[...]
    a = jnp.dot(h, wa_bf[...], preferred_element_type=F32)
    b = jnp.dot(h, wb_bf[...], preferred_element_type=F32)
    if mode == "glu":
        y = a * jax.nn.sigmoid(b)
    else:
        y = a * jax.nn.sigmoid(a) * b
    o_ref[...] = y.astype(o_ref.dtype)


def project_gated(h, wa, wb, n_cols, *, mode):
    M, K = h.shape
    bm, bn = PROJ_TILE_GATED
    assert M % bm == 0 and n_cols % bn == 0
    return pl.pallas_call(
        functools.partial(_proj_gated_kernel, mode=mode),
        grid=(n_cols // bn, M // bm),
        in_specs=[pl.BlockSpec((bm, K), lambda j, i: (i, 0)), _weight_spec(*wa, bn), _weight_spec(*wb, bn)],
        out_specs=pl.BlockSpec((bm, bn), lambda j, i: (i, j)),
        out_shape=jax.ShapeDtypeStruct((M, n_cols), BF16),
        scratch_shapes=[pltpu.VMEM((K, bn), BF16), pltpu.VMEM((K, bn), BF16)],
        compiler_params=_params(2),
        name="project_" + mode,
    )(h, wa[0], wb[0])


def _down_kernel(*refs, emit_next):
    if emit_next:
        a_ref, w_ref, x_ref, g_ref, gn_ref, o_ref, hn_ref = refs
    else:
        a_ref, w_ref, x_ref, g_ref, o_ref = refs
    y = jnp.dot(a_ref[...], w_ref[...], preferred_element_type=F32)
    xn = x_ref[...] + _rms(y, g_ref[...])
    o_ref[...] = xn
    if emit_next:
        hn_ref[...] = _rms(xn, gn_ref[...]).astype(BF16)


def down_residual(a, w, x, g, g_next):
    M, K = a.shape
    N = w.shape[1]
    bm = ROW_TILE
    emit_next = g_next is not None
    assert M % bm == 0
    tile = pl.BlockSpec((bm, N), lambda i: (i, 0))
    in_specs = [pl.BlockSpec((bm, K), lambda i: (i, 0)), _resident((K, N)), tile, _resident((1, N))]
    args = [a, w, x, g.reshape(1, N)]
    out_specs, out_shape = tile, jax.ShapeDtypeStruct((M, N), F32)
    if emit_next:
        in_specs.append(_resident((1, N)))
        args.append(g_next.reshape(1, N))
        out_specs, out_shape = (tile, tile), (out_shape, jax.ShapeDtypeStruct((M, N), BF16))
    res = pl.pallas_call(
        functools.partial(_down_kernel, emit_next=emit_next),
        grid=(M // bm,),
        in_specs=in_specs,
        out_specs=out_specs,
        out_shape=out_shape,
        compiler_params=_params(1),
        name="down_residual",
    )(*args)
    return res if emit_next else (res, None)


CONV_ROWS = 128
CONV_LN_ROWS = 8
CONV_LN_INTERLEAVE = 16
CONV_DATA_OFF = 16


def _conv_kernel(h_ref, wdw_ref, bdw_ref, lng_ref, lnb_ref, o_ref, hp_ref, cv_ref, *, S, C, KW):
    pad = KW // 2
    off = CONV_DATA_OFF
    n_slab = C // LANES
    R = CONV_ROWS

    for j in range(n_slab):
        cols = slice(j * LANES, (j + 1) * LANES)
        hp_ref[j, 0:off, :] = jnp.zeros((off, LANES), F32)
        hp_ref[j, off + S:off + S + off, :] = jnp.zeros((off, LANES), F32)

        def copy(i, carry, j=j, cols=cols):
            r = pl.multiple_of(i * 256, 256)
            hp_ref[j, pl.ds(off + r, 256), :] = h_ref[pl.ds(r, 256), cols].astype(F32)
            return carry
        lax.fori_loop(0, S // 256, copy, 0)

        def block(i, carry, j=j, cols=cols):
            for phase in range(2):
                t0 = i * (2 * R) + phase
                acc = jnp.broadcast_to(bdw_ref[:, cols], (R, LANES))
                for k in range(KW):
                    taps = hp_ref[j, pl.ds(t0 + (off - pad + k), R, stride=2), :]
                    acc = acc + taps * wdw_ref[k:k + 1, cols]
                cv_ref[j, pl.ds(t0, R, stride=2), :] = acc
            return carry
        lax.fori_loop(0, S // (2 * R), block, 0)

    def norm(i, carry):
        for u in range(CONV_LN_INTERLEAVE):
            start = (i * CONV_LN_INTERLEAVE + u) * CONV_LN_ROWS
            rows = pl.ds(pl.multiple_of(start, CONV_LN_ROWS), CONV_LN_ROWS)
            xs = [cv_ref[j, rows, :] for j in range(n_slab)]
            mu = jnp.sum(sum(xs), axis=-1, keepdims=True) * (1.0 / C)
            xs = [x - mu for x in xs]
            var = jnp.sum(sum(x * x for x in xs), axis=-1, keepdims=True) * (1.0 / C)
            inv = lax.rsqrt(var + EPS)
            for j in range(n_slab):
                cols = slice(j * LANES, (j + 1) * LANES)
                y = xs[j] * inv * lng_ref[:, cols] + lnb_ref[:, cols]
                o_ref[rows, cols] = (y * jax.nn.sigmoid(y)).astype(o_ref.dtype)
        return carry
    lax.fori_loop(0, S // (CONV_LN_ROWS * CONV_LN_INTERLEAVE), norm, 0)


def conv_branch(h, w_dw, b_dw, ln_g, ln_b):
    B, S, C = h.shape
    KW = w_dw.shape[0]
    assert KW // 2 < CONV_DATA_OFF and S % (2 * CONV_ROWS) == 0 and C % LANES == 0
    row = lambda v: v.reshape(1, C)
    vec = pl.BlockSpec((1, C), lambda b: (0, 0))
    return pl.pallas_call(
        functools.partial(_conv_kernel, S=S, C=C, KW=KW),
        grid=(B,),
        in_specs=[
            pl.BlockSpec((None, S, C), lambda b: (b, 0, 0)),
            pl.BlockSpec((KW, C), lambda b: (0, 0)),
            vec, vec, vec,
        ],
        out_specs=pl.BlockSpec((None, S, C), lambda b: (b, 0, 0)),
        out_shape=jax.ShapeDtypeStruct((B, S, C), BF16),
        scratch_shapes=[pltpu.VMEM((C // LANES, S + 2 * CONV_DATA_OFF, LANES), F32),
                        pltpu.VMEM((C // LANES, S, LANES), F32)],
        compiler_params=_params(1),
        name="conv_branch",
    )(h, w_dw, row(b_dw), row(ln_g), row(ln_b))


ATTN_Q_ROWS = 128


def _attn_kernel(q0, k0, v0, q1, k1, v1, q2, k2, v2, o_ref,
                 qf, kf, vf, qs, kp, vp, og0, og1, og2, lg0, lg1, lg2, bias_ref, s_buf, p_buf, rd_buf, *, S):
    T = ATTN_Q_ROWS
    HB = BAND_HALF
    W = T + 2 * HB
    D = HEAD_DIM
    n_blocks = S // T
    qkv = ((q0, k0, v0), (q1, k1, v1), (q2, k2, v2))
    outs = ((og0, lg0), (og1, lg1), (og2, lg2))
    slot_f = jnp.full((T, W), pl.program_id(1), jnp.int32).astype(F32)

    a_idx = lax.broadcasted_iota(jnp.int32, (T, W), 0)
    c_idx = lax.broadcasted_iota(jnp.int32, (T, W), 1)
    rel = jnp.abs(c_idx - HB - a_idx)
    c_row = lax.broadcasted_iota(jnp.int32, (1, W), 1)

    for g, (_, d) in enumerate(DIL_GROUPS):
        L = S // d
        P = L + 2 * HB
        nb = L // T
        q_ref, k_ref, v_ref = qkv[g]
        og, lg = outs[g]

        slope = jnp.exp2(-8.0 * (slot_f + (HEADS_PER_GROUP * g + 1.0)) / N_ATTN_HEADS)
        bias_ref[...] = jnp.where(rel <= HB, -(slope * (d * rel).astype(F32)), NEG_INF)

        kp[0:HB, :] = jnp.zeros((HB, D), BF16)
        vp[0:HB, :] = jnp.zeros((HB, D), BF16)

        def clear(r, carry, L=L, P=P):
            r0 = pl.multiple_of(r * P + (HB + L), HB)
            kp[pl.ds(r0, 2 * HB), :] = jnp.zeros((2 * HB, D), BF16)
            vp[pl.ds(r0, 2 * HB), :] = jnp.zeros((2 * HB, D), BF16)
            return carry
        lax.fori_loop(0, d, clear, 0)

        if d == 1:
            def copy(i, carry, k_ref=k_ref, v_ref=v_ref):
                r0 = pl.multiple_of(i * 256, 256)
                kp[pl.ds(HB + r0, 256), :] = k_ref[pl.ds(r0, 256), :]
                vp[pl.ds(HB + r0, 256), :] = v_ref[pl.ds(r0, 256), :]
                return carry
            lax.fori_loop(0, S // 256, copy, 0)
            q_src = q_ref
        else:
            def widen(i, carry, q_ref=q_ref, k_ref=k_ref, v_ref=v_ref):
                r0 = pl.multiple_of(i * 256, 256)
                qf[pl.ds(r0, 256), :] = q_ref[pl.ds(r0, 256), :].astype(F32)
                kf[pl.ds(r0, 256), :] = k_ref[pl.ds(r0, 256), :].astype(F32)
                vf[pl.ds(r0, 256), :] = v_ref[pl.ds(r0, 256), :].astype(F32)
                return carry
            lax.fori_loop(0, S // 256, widen, 0)

            def gather(r, carry, d=d, L=L, P=P):
                for c in range(L // T):
                    src = pl.ds(r + d * c * T, T, stride=d)
                    qs[pl.ds(pl.multiple_of(r * L + c * T, T), T), :] = qf[src, :].astype(BF16)
                    dst = pl.ds(pl.multiple_of(r * P + (HB + c * T), HB), T)
                    kp[dst, :] = kf[src, :].astype(BF16)
                    vp[dst, :] = vf[src, :].astype(BF16)
                return carry
            lax.fori_loop(0, d, gather, 0, unroll=2)
            q_src = qs

        def geometry(n):
            r, i = divmod(n, nb)
            win = slice(r * P + i * T, r * P + i * T + W)
            dst = slice(n * T, (n + 1) * T) if d == 1 else pl.ds(r + d * i * T, T, stride=d)
            return i, win, dst

        for n in range(n_blocks):
            i, win, _ = geometry(n)
            s = lax.dot_general(q_src[n * T:(n + 1) * T, :], kp[win, :], (((1,), (1,)), ((), ())),
                                preferred_element_type=F32)
            s = s + bias_ref[...]
            kidx = c_row + (i * T - HB)
            if i == 0:
                s = s + jnp.where(kidx >= 0, 0.0, NEG_INF)
            if i == nb - 1:
                s = s + jnp.where(kidx < L, 0.0, NEG_INF)
            s_buf[n] = s
        for n in range(n_blocks):
            _, _, dst = geometry(n)
            m = jnp.max(s_buf[n], axis=-1, keepdims=True)
            p = jnp.exp(s_buf[n] - m)
            den = jnp.sum(p, axis=-1, keepdims=True)
            p_buf[n] = p.astype(BF16)
            rd_buf[n] = jnp.broadcast_to(1.0 / den, (T, D))
            lg[dst, :] = jnp.broadcast_to(m + jnp.log(den), (T, D))
        for n in range(n_blocks):
            _, win, dst = geometry(n)
            og[dst, :] = jnp.dot(p_buf[n], vp[win, :], preferred_element_type=F32) * rd_buf[n]

    def mix(i, carry):
        r0 = pl.multiple_of(i * 256, 256)
        rows = pl.ds(r0, 256)
        l0, l1, l2 = lg0[rows, :], lg1[rows, :], lg2[rows, :]
        m = jnp.maximum(jnp.maximum(l0, l1), l2)
        w0, w1, w2 = jnp.exp(l0 - m), jnp.exp(l1 - m), jnp.exp(l2 - m)
        y = (w0 * og0[rows, :] + w1 * og1[rows, :] + w2 * og2[rows, :]) / (w0 + w1 + w2)
        o_ref[rows, :] = y.astype(o_ref.dtype)
        return carry
    lax.fori_loop(0, S // 256, mix, 0)


def dilated_mixture_attention(qkv):
    B, S, _ = qkv.shape
    D = HEAD_DIM
    T = ATTN_Q_ROWS
    assert all(S % (d * T) == 0 and w // (2 * d) == BAND_HALF for w, d in DIL_GROUPS)
    assert S % 256 == 0
    max_d = max(d for _, d in DIL_GROUPS)

    def head_spec(part, g):
        return pl.BlockSpec((None, S, D),
                            lambda b, j: (b, 0, part * N_ATTN_HEADS + HEADS_PER_GROUP * g + j))
    in_specs = [head_spec(part, g) for g in range(len(DIL_GROUPS)) for part in range(3)]
    seq_f32 = pltpu.VMEM((S, D), F32)
    return pl.pallas_call(
        functools.partial(_attn_kernel, S=S),
        grid=(B, HEADS_PER_GROUP),
        in_specs=in_specs,
        out_specs=pl.BlockSpec((None, S, D), lambda b, j: (b, 0, j)),
        out_shape=jax.ShapeDtypeStruct((B, S, HEADS_PER_GROUP * D), BF16),
        scratch_shapes=[
            seq_f32, seq_f32, seq_f32,
            pltpu.VMEM((S, D), BF16),
            pltpu.VMEM((S + 2 * BAND_HALF * max_d + BAND_HALF, D), BF16),
            pltpu.VMEM((S + 2 * BAND_HALF * max_d + BAND_HALF, D), BF16),
            seq_f32, seq_f32, seq_f32,
            seq_f32, seq_f32, seq_f32,
            pltpu.VMEM((T, T + 2 * BAND_HALF), F32),
            pltpu.VMEM((S // T, T, T + 2 * BAND_HALF), F32),
            pltpu.VMEM((S // T, T, T + 2 * BAND_HALF), BF16),
            pltpu.VMEM((S // T, T, D), F32),
        ],
        compiler_params=_params(2),
        name="dilated_attention",
    )(*([qkv] * 9))


def _merge_kernel(c_ref, a_ref, ga_ref, gb_ref, x_ref, wc_ref, wa_ref, wo_ref, g_ref, o_ref):
    yc = jnp.dot(c_ref[...], wc_ref[...], preferred_element_type=F32)
    ya = jnp.dot(a_ref[...], wa_ref[...], preferred_element_type=F32)
    merged = ga_ref[...].astype(F32) * yc + gb_ref[...].astype(F32) * ya
    y = jnp.dot(merged.astype(BF16), wo_ref[...], preferred_element_type=F32)
    o_ref[...] = x_ref[...] + _rms(y, g_ref[...])


def merge_out(c, a, gates, x, wc, wa, wo, g):
    M, D = x.shape
    Cc, Ca = c.shape[1], a.shape[1]
    bm = ROW_TILE
    assert M % bm == 0
    return pl.pallas_call(
        _merge_kernel,
        grid=(M // bm,),
        in_specs=[
            pl.BlockSpec((bm, Cc), lambda i: (i, 0)),
            pl.BlockSpec((bm, Ca), lambda i: (i, 0)),
            pl.BlockSpec((bm, D), lambda i: (i, 0)),
            pl.BlockSpec((bm, D), lambda i: (i, 1)),
            pl.BlockSpec((bm, D), lambda i: (i, 0)),
            _resident((Cc, D)), _resident((Ca, D)), _resident((D, D)), _resident((1, D)),
        ],
        out_specs=pl.BlockSpec((bm, D), lambda i: (i, 0)),
        out_shape=jax.ShapeDtypeStruct((M, D), F32),
        compiler_params=_params(1),
        name="merge_out",
    )(c, a, gates, gates, x, wc, wa, wo, g.reshape(1, D))


def _cross_kernel(x_ref, k_ref, v_ref, wq_ref, wo_ref, gpre_ref, gpost_ref, gn_ref, o_ref, hn_ref, *, D):
    dh = D // X_HEADS

    heads = [slice(hd * dh, (hd + 1) * dh) for hd in range(X_HEADS)]
    h = _rms(x_ref[...], gpre_ref[...]).astype(BF16)
    q = (jnp.dot(h, wq_ref[...], preferred_element_type=F32) * (dh ** -0.5)).astype(BF16)
    scores = [lax.dot_general(q[:, cols], k_ref[:, cols], (((1,), (1,)), ((), ())),
                              preferred_element_type=F32) for cols in heads]
    probs = []
    for s in scores:
        p = jnp.exp(s - jnp.max(s, axis=-1, keepdims=True))
        probs.append((p * (1.0 / jnp.sum(p, axis=-1, keepdims=True))).astype(BF16))
    ctx = [jnp.dot(p, v_ref[:, cols], preferred_element_type=F32).astype(BF16)
           for p, cols in zip(probs, heads)]
    y = jnp.dot(jnp.concatenate(ctx, axis=-1), wo_ref[...], preferred_element_type=F32)
    xn = x_ref[...] + _rms(y, gpost_ref[...])
    o_ref[...] = xn
    hn_ref[...] = _rms(xn, gn_ref[...]).astype(BF16)


def cross_attention_sublayer(x, kmem, vmem, wq, wo, g_pre, g_post, g_next):
    B, S, D = x.shape
    Nm = kmem.shape[1]
    bm = ROW_TILE
    assert S % bm == 0
    tile = pl.BlockSpec((None, bm, D), lambda b, i: (b, i, 0))
    mem_spec = pl.BlockSpec((None, Nm, D), lambda b, i: (b, 0, 0))
    vec = lambda g: g.reshape(1, D)
    return pl.pallas_call(
        functools.partial(_cross_kernel, D=D),
        grid=(B, S // bm),
        in_specs=[tile, mem_spec, mem_spec, _resident((D, D)), _resident((D, D)),
                  _resident((1, D)), _resident((1, D)), _resident((1, D))],
        out_specs=(tile, tile),
        out_shape=(jax.ShapeDtypeStruct((B, S, D), F32), jax.ShapeDtypeStruct((B, S, D), BF16)),
        compiler_params=_params(2),
        name="cross_attention",
    )(x, kmem, vmem, wq, wo, vec(g_pre), vec(g_post), vec(g_next))


def kernel(x, mem, w_in, w_dw, b_dw, conv_ln_g, conv_ln_b, w_conv_out, w_attn_proj, w_o, g_mix_pre, g_mix_post, g_mem, w_cq, w_ck, w_cv, w_co, g_x_pre, g_x_post, w_ffn_gate, w_ffn_up, w_ffn_down, g_ffn_pre, g_ffn_post):
    B, S, D = x.shape
    depth = w_in.shape[0]
    Cc = w_dw.shape[-1]
    Wq = N_ATTN_HEADS * HEAD_DIM
    Nm = mem.shape[1]
    d_ff = w_ffn_gate.shape[-1]
    c0 = 2 * Cc
    c3 = c0 + 3 * Wq
    M = B * S

    q_scale = jnp.concatenate([jnp.full((Wq,), HEAD_DIM ** -0.5, F32), jnp.ones((2 * Wq,), F32)])
    mem2 = mem.reshape(B * Nm, D)
    x2 = x.reshape(M, D)
    bf = lambda w: w.astype(BF16)

    h = rms_norm_bf16(x2, g_mix_pre[0])
    for l in range(depth):
        glu = project_gated(h, (w_in, l, 0), (w_in, l, Cc), Cc, mode="glu")
        qkv = project(h, w_in, l, c0, 3 * Wq, PROJ_TILE_QKV, act="colscale", colscale=q_scale)
        gates = project(h, w_in, l, c3, 2 * D, PROJ_TILE_WIDE, act="sigmoid")
        conv = conv_branch(glu.reshape(B, S, Cc), w_dw[l], b_dw[l], conv_ln_g[l], conv_ln_b[l])
        attn = dilated_mixture_attention(qkv.reshape(B, S, 3 * Wq))
        x2 = merge_out(conv.reshape(M, Cc), attn.reshape(M, -1), gates, x2,
                       bf(w_conv_out[l]), bf(w_attn_proj[l]), bf(w_o[l]), g_mix_post[l])
        memn = rms_norm_bf16(mem2, g_mem[l])
        kmem = project(memn, w_ck, l, 0, D, PROJ_TILE_WIDE)
        vmem = project(memn, w_cv, l, 0, D, PROJ_TILE_WIDE)
        x3, h3 = cross_attention_sublayer(x2.reshape(B, S, D), kmem.reshape(B, Nm, D), vmem.reshape(B, Nm, D),
                                          bf(w_cq[l]), bf(w_co[l]), g_x_pre[l], g_x_post[l], g_ffn_pre[l])
        x2, h = x3.reshape(M, D), h3.reshape(M, D)
        act = project_gated(h, (w_ffn_gate, l, 0), (w_ffn_up, l, 0), d_ff, mode="swiglu")
        x2, h = down_residual(act, bf(w_ffn_down[l]), x2, g_ffn_post[l],
                              g_mix_pre[l + 1] if l + 1 < depth else None)
    return x2.reshape(B, S, D)
```

```python
import functools

import jax
import jax.numpy as jnp
from jax import lax
from jax.experimental import pallas as pl
from jax.experimental.pallas import tpu as pltpu

F32 = jnp.float32
BF16 = jnp.bfloat16

EPS = 1e-6
NEG_INF = -1e30

HEAD_DIM = 128
DIL_GROUPS = ((128, 1), (512, 4), (2048, 16))
HEADS_PER_GROUP = 4
N_ATTN_HEADS = HEADS_PER_GROUP * len(DIL_GROUPS)
BAND_HALF = 64
X_HEADS = 4

V7X_VMEM_LIMIT_BYTES = 56 * 1024 * 1024
LANES = 128

PROJ_TILE_GATED = (1024, 512)
PROJ_TILE_QKV = (2048, 768)
PROJ_TILE_WIDE = (1024, 1024)
ROW_TILE = 256
NORM_ROWS = 1024


def _params(n_grid_dims):
    return pltpu.CompilerParams(
        dimension_semantics=("arbitrary",) * n_grid_dims,
        vmem_limit_bytes=V7X_VMEM_LIMIT_BYTES)


def _resident(shape):
    return pl.BlockSpec(shape, lambda *_: (0,) * len(shape), pipeline_mode=pl.Buffered(1))


def _rms(y, g):
    ms = jnp.mean(y * y, axis=-1, keepdims=True)
    return y * lax.rsqrt(ms + EPS) * g


WEIGHT_CHUNK_ROWS = 256
_HBM = pl.BlockSpec(memory_space=pl.ANY)


def _stage_scratch(n_cols):
    return [pltpu.VMEM((2, WEIGHT_CHUNK_ROWS, n_cols), F32), pltpu.SemaphoreType.DMA((2,))]


def _load_weight_bf16(w_hbm, layer, wb_ref, stage_ref, sem):
    R = WEIGHT_CHUNK_ROWS
    n_chunks = wb_ref.shape[0] // R
    assert wb_ref.shape[0] % R == 0

    def copy(c):
        return pltpu.make_async_copy(w_hbm.at[layer, pl.ds(c * R, R), :], stage_ref.at[c % 2], sem.at[c % 2])

    copy(0).start()
    for c in range(n_chunks):
        if c + 1 < n_chunks:
            copy(c + 1).start()
        copy(c).wait()
        wb_ref[c * R:(c + 1) * R, :] = stage_ref[c % 2].astype(BF16)


def _norm_kernel(x_ref, g_ref, o_ref, *, bm, chunk=64):
    def body(i, carry):
        r = pl.multiple_of(i * chunk, chunk)
        o_ref[pl.ds(r, chunk), :] = _rms(x_ref[pl.ds(r, chunk), :], g_ref[...]).astype(BF16)
        return carry
    lax.fori_loop(0, bm // chunk, body, 0)


def rms_norm_bf16(x, g):
    M, K = x.shape
    bm = NORM_ROWS
    assert M % bm == 0
    return pl.pallas_call(
        functools.partial(_norm_kernel, bm=bm),
        grid=(M // bm,),
        in_specs=[pl.BlockSpec((bm, K), lambda i: (i, 0)), pl.BlockSpec((1, K), lambda i: (0, 0))],
        out_specs=pl.BlockSpec((bm, K), lambda i: (i, 0)),
        out_shape=jax.ShapeDtypeStruct((M, K), BF16),
        compiler_params=_params(1),
        name="rms_norm",
    )(x, g.reshape(1, K))


def _weight_spec(w_stack, layer, col0, bn):
    K = w_stack.shape[1]
    assert col0 % LANES == 0 and bn % LANES == 0
    return pl.BlockSpec((pl.Element(1), pl.Element(K), pl.Element(bn)),
                        lambda j, i: (layer, 0, pl.multiple_of(col0 + j * bn, LANES)))


def _proj_kernel(*refs, act):
    if act == "colscale":
        h_ref, w_ref, cs_ref, o_ref, wb_ref = refs
    else:
        h_ref, w_ref, o_ref, wb_ref = refs

    @pl.when(pl.program_id(1) == 0)
    def _():
        wb_ref[...] = w_ref[0].astype(BF16)

    acc = jnp.dot(h_ref[...], wb_ref[...], preferred_element_type=F32)
    if act == "sigmoid":
        acc = jax.nn.sigmoid(acc)
    elif act == "colscale":
        acc = acc * cs_ref[...]
    o_ref[...] = acc.astype(o_ref.dtype)


def project(h, w_stack, layer, col0, n_cols, tile, *, act="none", colscale=None):
    M, K = h.shape
    bm, bn = tile
    assert M % bm == 0 and n_cols % bn == 0 and (act == "colscale") == (colscale is not None)
    in_specs = [pl.BlockSpec((bm, K), lambda j, i: (i, 0)), _weight_spec(w_stack, layer, col0, bn)]
    args = [h, w_stack]
    if colscale is not None:
        in_specs.append(pl.BlockSpec((1, bn), lambda j, i: (0, j)))
        args.append(colscale.reshape(1, n_cols))
    return pl.pallas_call(
        functools.partial(_proj_kernel, act=act),
        grid=(n_cols // bn, M // bm),
        in_specs=in_specs,
        out_specs=pl.BlockSpec((bm, bn), lambda j, i: (i, j)),
        out_shape=jax.ShapeDtypeStruct((M, n_cols), BF16),
        scratch_shapes=[pltpu.VMEM((K, bn), BF16)],
        compiler_params=_params(2),
        name="project_" + act,
    )(*args)


def _proj_gated_kernel(h_ref, wa_ref, wb_ref, o_ref, wa_bf, wb_bf, *, mode):
    @pl.when(pl.program_id(1) == 0)
    def _():
        wa_bf[...] = wa_ref[0].astype(BF16)
        wb_bf[...] = wb_ref[0].astype(BF16)

    h = h_ref[...]
    a = jnp.dot(h, wa_bf[...], preferred_element_type=F32)
    b = jnp.dot(h, wb_bf[...], preferred_element_type=F32)
    if mode == "glu":
        y = a * jax.nn.sigmoid(b)
    else:
        y = a * jax.nn.sigmoid(a) * b
    o_ref[...] = y.astype(o_ref.dtype)


def project_gated(h, wa, wb, n_cols, *, mode):
    M, K = h.shape
    bm, bn = PROJ_TILE_GATED
    assert M % bm == 0 and n_cols % bn == 0
    return pl.pallas_call(
        functools.partial(_proj_gated_kernel, mode=mode),
        grid=(n_cols // bn, M // bm),
        in_specs=[pl.BlockSpec((bm, K), lambda j, i: (i, 0)), _weight_spec(*wa, bn), _weight_spec(*wb, bn)],
        out_specs=pl.BlockSpec((bm, bn), lambda j, i: (i, j)),
        out_shape=jax.ShapeDtypeStruct((M, n_cols), BF16),
        scratch_shapes=[pltpu.VMEM((K, bn), BF16), pltpu.VMEM((K, bn), BF16)],
        compiler_params=_params(2),
        name="project_" + mode,
    )(h, wa[0], wb[0])


MEM_KV_COLS = 256


def _mem_kv_kernel(mem_ref, g_ref, wk_ref, wv_ref, k_ref, v_ref, memn_ref, *, rows, chunk=64):
    @pl.when(pl.program_id(0) == 0)
    def _():
        def body(i, carry):
            r = pl.multiple_of(i * chunk, chunk)
            memn_ref[pl.ds(r, chunk), :] = _rms(mem_ref[pl.ds(r, chunk), :], g_ref[...]).astype(BF16)
            return carry
        lax.fori_loop(0, rows // chunk, body, 0)

    memn = memn_ref[...]
    k_ref[...] = jnp.dot(memn, wk_ref[0].astype(BF16), preferred_element_type=F32).astype(BF16)
    v_ref[...] = jnp.dot(memn, wv_ref[0].astype(BF16), preferred_element_type=F32).astype(BF16)


def memory_kv(mem, g, wk_stack, wv_stack, layer):
    R, K = mem.shape
    N = wk_stack.shape[2]
    bn = MEM_KV_COLS
    assert N % bn == 0 and R % 64 == 0
    w_spec = pl.BlockSpec((pl.Element(1), pl.Element(K), pl.Element(bn)),
                          lambda j: (layer, 0, pl.multiple_of(j * bn, LANES)))
    out_spec = pl.BlockSpec((R, bn), lambda j: (0, j))
    out = jax.ShapeDtypeStruct((R, N), BF16)
    return pl.pallas_call(
        functools.partial(_mem_kv_kernel, rows=R),
        grid=(N // bn,),
        in_specs=[_resident((R, K)), _resident((1, K)), w_spec, w_spec],
        out_specs=(out_spec, out_spec),
        out_shape=(out, out),
        scratch_shapes=[pltpu.VMEM((R, K), BF16)],
        compiler_params=_params(1),
        name="memory_kv",
    )(mem, g.reshape(1, K), wk_stack, wv_stack)


def _down_kernel(*refs, layer, emit_next):
    if emit_next:
        a_ref, w_hbm, x_ref, g_ref, gn_ref, o_ref, hn_ref, w_ref, stage_ref, sem = refs
    else:
        a_ref, w_hbm, x_ref, g_ref, o_ref, w_ref, stage_ref, sem = refs

    @pl.when(pl.program_id(0) == 0)
    def _():
        _load_weight_bf16(w_hbm, layer, w_ref, stage_ref, sem)

    y = jnp.dot(a_ref[...], w_ref[...], preferred_element_type=F32)
    xn = x_ref[...] + _rms(y, g_ref[...])
    o_ref[...] = xn
    if emit_next:
        hn_ref[...] = _rms(xn, gn_ref[...]).astype(BF16)


def down_residual(a, w_stack, layer, x, g, g_next):
    M, K = a.shape
    N = w_stack.shape[2]
    bm = ROW_TILE
    emit_next = g_next is not None
    assert M % bm == 0
    tile = pl.BlockSpec((bm, N), lambda i: (i, 0))
    in_specs = [pl.BlockSpec((bm, K), lambda i: (i, 0)), _HBM, tile, _resident((1, N))]
    args = [a, w_stack, x, g.reshape(1, N)]
    out_specs, out_shape = tile, jax.ShapeDtypeStruct((M, N), F32)
    if emit_next:
        in_specs.append(_resident((1, N)))
        args.append(g_next.reshape(1, N))
        out_specs, out_shape = (tile, tile), (out_shape, jax.ShapeDtypeStruct((M, N), BF16))
    res = pl.pallas_call(
        functools.partial(_down_kernel, layer=layer, emit_next=emit_next),
        grid=(M // bm,),
        in_specs=in_specs,
        out_specs=out_specs,
        out_shape=out_shape,
        scratch_shapes=[pltpu.VMEM((K, N), BF16)] + _stage_scratch(N),
        compiler_params=_params(1),
        name="down_residual",
    )(*args)
    return res if emit_next else (res, None)


CONV_ROWS = 128
CONV_LN_ROWS = 8
CONV_LN_INTERLEAVE = 16
CONV_DATA_OFF = 16


def _conv_kernel(h_ref, wdw_ref, bdw_ref, lng_ref, lnb_ref, o_ref, hp_ref, cv_ref, *, S, C, KW):
    pad = KW // 2
    off = CONV_DATA_OFF
    n_slab = C // LANES
    R = CONV_ROWS

    for j in range(n_slab):
        cols = slice(j * LANES, (j + 1) * LANES)
        hp_ref[j, 0:off, :] = jnp.zeros((off, LANES), F32)
        hp_ref[j, off + S:off + S + off, :] = jnp.zeros((off, LANES), F32)

        def copy(i, carry, j=j, cols=cols):
            r = pl.multiple_of(i * 256, 256)
            hp_ref[j, pl.ds(off + r, 256), :] = h_ref[pl.ds(r, 256), cols].astype(F32)
            return carry
        lax.fori_loop(0, S // 256, copy, 0)

        def block(i, carry, j=j, cols=cols):
            for phase in range(2):
                t0 = i * (2 * R) + phase
                acc = jnp.broadcast_to(bdw_ref[:, cols], (R, LANES))
                for k in range(KW):
                    taps = hp_ref[j, pl.ds(t0 + (off - pad + k), R, stride=2), :]
                    acc = acc + taps * wdw_ref[k:k + 1, cols]
                cv_ref[j, pl.ds(t0, R, stride=2), :] = acc
            return carry
        lax.fori_loop(0, S // (2 * R), block, 0)

    def norm(i, carry):
        for u in range(CONV_LN_INTERLEAVE):
            start = (i * CONV_LN_INTERLEAVE + u) * CONV_LN_ROWS
            rows = pl.ds(pl.multiple_of(start, CONV_LN_ROWS), CONV_LN_ROWS)
            xs = [cv_ref[j, rows, :] for j in range(n_slab)]
            mu = jnp.sum(sum(xs), axis=-1, keepdims=True) * (1.0 / C)
            xs = [x - mu for x in xs]
            var = jnp.sum(sum(x * x for x in xs), axis=-1, keepdims=True) * (1.0 / C)
            inv = lax.rsqrt(var + EPS)
            for j in range(n_slab):
                cols = slice(j * LANES, (j + 1) * LANES)
                y = xs[j] * inv * lng_ref[:, cols] + lnb_ref[:, cols]
                o_ref[rows, cols] = (y * jax.nn.sigmoid(y)).astype(o_ref.dtype)
        return carry
    lax.fori_loop(0, S // (CONV_LN_ROWS * CONV_LN_INTERLEAVE), norm, 0)


def conv_branch(h, w_dw, b_dw, ln_g, ln_b):
    B, S, C = h.shape
    KW = w_dw.shape[0]
    assert KW // 2 < CONV_DATA_OFF and S % (2 * CONV_ROWS) == 0 and C % LANES == 0
    row = lambda v: v.reshape(1, C)
    vec = pl.BlockSpec((1, C), lambda b: (0, 0))
    return pl.pallas_call(
        functools.partial(_conv_kernel, S=S, C=C, KW=KW),
        grid=(B,),
        in_specs=[
            pl.BlockSpec((None, S, C), lambda b: (b, 0, 0)),
            pl.BlockSpec((KW, C), lambda b: (0, 0)),
            vec, vec, vec,
        ],
        out_specs=pl.BlockSpec((None, S, C), lambda b: (b, 0, 0)),
        out_shape=jax.ShapeDtypeStruct((B, S, C), BF16),
        scratch_shapes=[pltpu.VMEM((C // LANES, S + 2 * CONV_DATA_OFF, LANES), F32),
                        pltpu.VMEM((C // LANES, S, LANES), F32)],
        compiler_params=_params(1),
        name="conv_branch",
    )(h, w_dw, row(b_dw), row(ln_g), row(ln_b))


ATTN_Q_ROWS = 128


def _attn_kernel(q0, k0, v0, q1, k1, v1, q2, k2, v2, o_ref,
                 qf, kf, vf, qs, kp, vp, og0, og1, og2, lg0, lg1, lg2, bias_ref, s_buf, p_buf, rd_buf, *, S):
    T = ATTN_Q_ROWS
    HB = BAND_HALF
    W = T + 2 * HB
    D = HEAD_DIM
    n_blocks = S // T
    qkv = ((q0, k0, v0), (q1, k1, v1), (q2, k2, v2))
    outs = ((og0, lg0), (og1, lg1), (og2, lg2))
    slot_f = jnp.full((T, W), pl.program_id(1), jnp.int32).astype(F32)

    a_idx = lax.broadcasted_iota(jnp.int32, (T, W), 0)
    c_idx = lax.broadcasted_iota(jnp.int32, (T, W), 1)
    rel = jnp.abs(c_idx - HB - a_idx)
    c_row = lax.broadcasted_iota(jnp.int32, (1, W), 1)

    for g, (_, d) in enumerate(DIL_GROUPS):
        L = S // d
        P = L + 2 * HB
        nb = L // T
        q_ref, k_ref, v_ref = qkv[g]
        og, lg = outs[g]

        slope = jnp.exp2(-8.0 * (slot_f + (HEADS_PER_GROUP * g + 1.0)) / N_ATTN_HEADS)
        bias_ref[...] = jnp.where(rel <= HB, -(slope * (d * rel).astype(F32)), NEG_INF)

        kp[0:HB, :] = jnp.zeros((HB, D), BF16)
        vp[0:HB, :] = jnp.zeros((HB, D), BF16)

        def clear(r, carry, L=L, P=P):
            r0 = pl.multiple_of(r * P + (HB + L), HB)
            kp[pl.ds(r0, 2 * HB), :] = jnp.zeros((2 * HB, D), BF16)
            vp[pl.ds(r0, 2 * HB), :] = jnp.zeros((2 * HB, D), BF16)
            return carry
        lax.fori_loop(0, d, clear, 0)

        if d == 1:
            def copy(i, carry, k_ref=k_ref, v_ref=v_ref):
                r0 = pl.multiple_of(i * 256, 256)
                kp[pl.ds(HB + r0, 256), :] = k_ref[pl.ds(r0, 256), :]
                vp[pl.ds(HB + r0, 256), :] = v_ref[pl.ds(r0, 256), :]
                return carry
            lax.fori_loop(0, S // 256, copy, 0)
            q_src = q_ref
        else:
            def widen(i, carry, q_ref=q_ref, k_ref=k_ref, v_ref=v_ref):
                r0 = pl.multiple_of(i * 256, 256)
                qf[pl.ds(r0, 256), :] = q_ref[pl.ds(r0, 256), :].astype(F32)
                kf[pl.ds(r0, 256), :] = k_ref[pl.ds(r0, 256), :].astype(F32)
                vf[pl.ds(r0, 256), :] = v_ref[pl.ds(r0, 256), :].astype(F32)
                return carry
            lax.fori_loop(0, S // 256, widen, 0)

            def gather(r, carry, d=d, L=L, P=P):
                for c in range(L // T):
                    src = pl.ds(r + d * c * T, T, stride=d)
                    qs[pl.ds(pl.multiple_of(r * L + c * T, T), T), :] = qf[src, :].astype(BF16)
                    dst = pl.ds(pl.multiple_of(r * P + (HB + c * T), HB), T)
                    kp[dst, :] = kf[src, :].astype(BF16)
                    vp[dst, :] = vf[src, :].astype(BF16)
                return carry
            lax.fori_loop(0, d, gather, 0, unroll=2)
            q_src = qs

        def geometry(n):
            r, i = divmod(n, nb)
            win = slice(r * P + i * T, r * P + i * T + W)
            dst = slice(n * T, (n + 1) * T) if d == 1 else pl.ds(r + d * i * T, T, stride=d)
            return i, win, dst

        for n in range(n_blocks):
            i, win, _ = geometry(n)
            s = lax.dot_general(q_src[n * T:(n + 1) * T, :], kp[win, :], (((1,), (1,)), ((), ())),
                                preferred_element_type=F32)
            s = s + bias_ref[...]
            kidx = c_row + (i * T - HB)
            if i == 0:
                s = s + jnp.where(kidx >= 0, 0.0, NEG_INF)
            if i == nb - 1:
                s = s + jnp.where(kidx < L, 0.0, NEG_INF)
            s_buf[n] = s
        for n in range(n_blocks):
            _, _, dst = geometry(n)
            m = jnp.max(s_buf[n], axis=-1, keepdims=True)
            p = jnp.exp(s_buf[n] - m)
            den = jnp.sum(p, axis=-1, keepdims=True)
            p_buf[n] = p.astype(BF16)
            rd_buf[n] = jnp.broadcast_to(1.0 / den, (T, D))
            lg[dst, :] = jnp.broadcast_to(m + jnp.log(den), (T, D))
        for n in range(n_blocks):
            _, win, dst = geometry(n)
            og[dst, :] = jnp.dot(p_buf[n], vp[win, :], preferred_element_type=F32) * rd_buf[n]

    def mix(i, carry):
        r0 = pl.multiple_of(i * 256, 256)
        rows = pl.ds(r0, 256)
        l0, l1, l2 = lg0[rows, :], lg1[rows, :], lg2[rows, :]
        m = jnp.maximum(jnp.maximum(l0, l1), l2)
        w0, w1, w2 = jnp.exp(l0 - m), jnp.exp(l1 - m), jnp.exp(l2 - m)
        y = (w0 * og0[rows, :] + w1 * og1[rows, :] + w2 * og2[rows, :]) / (w0 + w1 + w2)
        o_ref[rows, :] = y.astype(o_ref.dtype)
        return carry
    lax.fori_loop(0, S // 256, mix, 0)


def dilated_mixture_attention(qkv):
    B, S, _ = qkv.shape
    D = HEAD_DIM
    T = ATTN_Q_ROWS
    assert all(S % (d * T) == 0 and w // (2 * d) == BAND_HALF for w, d in DIL_GROUPS)
    assert S % 256 == 0
    max_d = max(d for _, d in DIL_GROUPS)

    def head_spec(part, g):
        return pl.BlockSpec((None, S, D),
                            lambda b, j: (b, 0, part * N_ATTN_HEADS + HEADS_PER_GROUP * g + j))
    in_specs = [head_spec(part, g) for g in range(len(DIL_GROUPS)) for part in range(3)]
    seq_f32 = pltpu.VMEM((S, D), F32)
    return pl.pallas_call(
        functools.partial(_attn_kernel, S=S),
        grid=(B, HEADS_PER_GROUP),
        in_specs=in_specs,
        out_specs=pl.BlockSpec((None, S, D), lambda b, j: (b, 0, j)),
        out_shape=jax.ShapeDtypeStruct((B, S, HEADS_PER_GROUP * D), BF16),
        scratch_shapes=[
            seq_f32, seq_f32, seq_f32,
            pltpu.VMEM((S, D), BF16),
            pltpu.VMEM((S + 2 * BAND_HALF * max_d + BAND_HALF, D), BF16),
            pltpu.VMEM((S + 2 * BAND_HALF * max_d + BAND_HALF, D), BF16),
            seq_f32, seq_f32, seq_f32,
            seq_f32, seq_f32, seq_f32,
            pltpu.VMEM((T, T + 2 * BAND_HALF), F32),
            pltpu.VMEM((S // T, T, T + 2 * BAND_HALF), F32),
            pltpu.VMEM((S // T, T, T + 2 * BAND_HALF), BF16),
            pltpu.VMEM((S // T, T, D), F32),
        ],
        compiler_params=_params(2),
        name="dilated_attention",
    )(*([qkv] * 9))


def _merge_kernel(c_ref, a_ref, ga_ref, gb_ref, x_ref, wc_hbm, wa_hbm, wo_hbm, g_ref, o_ref,
                  wc_ref, wa_ref, wo_ref, stage_ref, sem, *, layer):
    @pl.when(pl.program_id(0) == 0)
    def _():
        _load_weight_bf16(wc_hbm, layer, wc_ref, stage_ref, sem)
        _load_weight_bf16(wa_hbm, layer, wa_ref, stage_ref, sem)
        _load_weight_bf16(wo_hbm, layer, wo_ref, stage_ref, sem)

    yc = jnp.dot(c_ref[...], wc_ref[...], preferred_element_type=F32)
    ya = jnp.dot(a_ref[...], wa_ref[...], preferred_element_type=F32)
    merged = ga_ref[...].astype(F32) * yc + gb_ref[...].astype(F32) * ya
    y = jnp.dot(merged.astype(BF16), wo_ref[...], preferred_element_type=F32)
    o_ref[...] = x_ref[...] + _rms(y, g_ref[...])


def merge_out(c, a, gates, x, wc, wa, wo, layer, g):
    M, D = x.shape
    Cc, Ca = c.shape[1], a.shape[1]
    bm = ROW_TILE
    assert M % bm == 0
    return pl.pallas_call(
        functools.partial(_merge_kernel, layer=layer),
        grid=(M // bm,),
        in_specs=[
            pl.BlockSpec((bm, Cc), lambda i: (i, 0)),
            pl.BlockSpec((bm, Ca), lambda i: (i, 0)),
            pl.BlockSpec((bm, D), lambda i: (i, 0)),
            pl.BlockSpec((bm, D), lambda i: (i, 1)),
            pl.BlockSpec((bm, D), lambda i: (i, 0)),
            _HBM, _HBM, _HBM, _resident((1, D)),
        ],
        out_specs=pl.BlockSpec((bm, D), lambda i: (i, 0)),
        out_shape=jax.ShapeDtypeStruct((M, D), F32),
        scratch_shapes=[pltpu.VMEM((Cc, D), BF16), pltpu.VMEM((Ca, D), BF16), pltpu.VMEM((D, D), BF16)]
        + _stage_scratch(D),
        compiler_params=_params(1),
        name="merge_out",
    )(c, a, gates, gates, x, wc, wa, wo, g.reshape(1, D))


def _cross_kernel(x_ref, k_ref, v_ref, wq_hbm, wo_hbm, gpre_ref, gpost_ref, gn_ref, o_ref, hn_ref,
                  wq_ref, wo_ref, stage_ref, sem, *, D, layer):
    dh = D // X_HEADS

    @pl.when((pl.program_id(0) == 0) & (pl.program_id(1) == 0))
    def _():
        _load_weight_bf16(wq_hbm, layer, wq_ref, stage_ref, sem)
        _load_weight_bf16(wo_hbm, layer, wo_ref, stage_ref, sem)

    heads = [slice(hd * dh, (hd + 1) * dh) for hd in range(X_HEADS)]
    h = _rms(x_ref[...], gpre_ref[...]).astype(BF16)
    q = (jnp.dot(h, wq_ref[...], preferred_element_type=F32) * (dh ** -0.5)).astype(BF16)
    scores = [lax.dot_general(q[:, cols], k_ref[:, cols], (((1,), (1,)), ((), ())),
                              preferred_element_type=F32) for cols in heads]
    probs = []
    for s in scores:
        p = jnp.exp(s - jnp.max(s, axis=-1, keepdims=True))
        probs.append((p * (1.0 / jnp.sum(p, axis=-1, keepdims=True))).astype(BF16))
    ctx = [jnp.dot(p, v_ref[:, cols], preferred_element_type=F32).astype(BF16)
           for p, cols in zip(probs, heads)]
    y = jnp.dot(jnp.concatenate(ctx, axis=-1), wo_ref[...], preferred_element_type=F32)
    xn = x_ref[...] + _rms(y, gpost_ref[...])
    o_ref[...] = xn
    hn_ref[...] = _rms(xn, gn_ref[...]).astype(BF16)


def cross_attention_sublayer(x, kmem, vmem, wq, wo, layer, g_pre, g_post, g_next):
    B, S, D = x.shape
    Nm = kmem.shape[1]
    bm = ROW_TILE
    assert S % bm == 0
    tile = pl.BlockSpec((None, bm, D), lambda b, i: (b, i, 0))
    mem_spec = pl.BlockSpec((None, Nm, D), lambda b, i: (b, 0, 0))
    vec = lambda g: g.reshape(1, D)
    return pl.pallas_call(
        functools.partial(_cross_kernel, D=D, layer=layer),
        grid=(B, S // bm),
        in_specs=[tile, mem_spec, mem_spec, _HBM, _HBM,
                  _resident((1, D)), _resident((1, D)), _resident((1, D))],
        out_specs=(tile, tile),
        out_shape=(jax.ShapeDtypeStruct((B, S, D), F32), jax.ShapeDtypeStruct((B, S, D), BF16)),
        scratch_shapes=[pltpu.VMEM((D, D), BF16), pltpu.VMEM((D, D), BF16)] + _stage_scratch(D),
        compiler_params=_params(2),
        name="cross_attention",
    )(x, kmem, vmem, wq, wo, vec(g_pre), vec(g_post), vec(g_next))


def kernel(x, mem, w_in, w_dw, b_dw, conv_ln_g, conv_ln_b, w_conv_out, w_attn_proj, w_o, g_mix_pre, g_mix_post, g_mem, w_cq, w_ck, w_cv, w_co, g_x_pre, g_x_post, w_ffn_gate, w_ffn_up, w_ffn_down, g_ffn_pre, g_ffn_post):
    B, S, D = x.shape
    depth = w_in.shape[0]
    Cc = w_dw.shape[-1]
    Wq = N_ATTN_HEADS * HEAD_DIM
    Nm = mem.shape[1]
    d_ff = w_ffn_gate.shape[-1]
    c0 = 2 * Cc
    c3 = c0 + 3 * Wq
    M = B * S

    q_scale = jnp.concatenate([jnp.full((Wq,), HEAD_DIM ** -0.5, F32), jnp.ones((2 * Wq,), F32)])
    mem2 = mem.reshape(B * Nm, D)
    x2 = x.reshape(M, D)
    h = rms_norm_bf16(x2, g_mix_pre[0])
    for l in range(depth):
        glu = project_gated(h, (w_in, l, 0), (w_in, l, Cc), Cc, mode="glu")
        qkv = project(h, w_in, l, c0, 3 * Wq, PROJ_TILE_QKV, act="colscale", colscale=q_scale)
        gates = project(h, w_in, l, c3, 2 * D, PROJ_TILE_WIDE, act="sigmoid")
        conv = conv_branch(glu.reshape(B, S, Cc), w_dw[l], b_dw[l], conv_ln_g[l], conv_ln_b[l])
        attn = dilated_mixture_attention(qkv.reshape(B, S, 3 * Wq))
        x2 = merge_out(conv.reshape(M, Cc), attn.reshape(M, -1), gates, x2,
                       w_conv_out, w_attn_proj, w_o, l, g_mix_post[l])
        kmem, vmem = memory_kv(mem2, g_mem[l], w_ck, w_cv, l)
        x3, h3 = cross_attention_sublayer(x2.reshape(B, S, D), kmem.reshape(B, Nm, D), vmem.reshape(B, Nm, D),
                                          w_cq, w_co, l, g_x_pre[l], g_x_post[l], g_ffn_pre[l])
        x2, h = x3.reshape(M, D), h3.reshape(M, D)
        act = project_gated(h, (w_ffn_gate, l, 0), (w_ffn_up, l, 0), d_ff, mode="swiglu")
        x2, h = down_residual(act, w_ffn_down, l, x2, g_ffn_post[l],
                              g_mix_pre[l + 1] if l + 1 < depth else None)
    return x2.reshape(B, S, D)
```

```python
import functools

import jax
import jax.numpy as jnp
from jax import lax
from jax.experimental import pallas as pl
from jax.experimental.pallas import tpu as pltpu

F32 = jnp.float32
BF16 = jnp.bfloat16

EPS = 1e-6
NEG_INF = -1e30

HEAD_DIM = 128
DIL_GROUPS = ((128, 1), (512, 4), (2048, 16))
HEADS_PER_GROUP = 4
N_ATTN_HEADS = HEADS_PER_GROUP * len(DIL_GROUPS)
BAND_HALF = 64
X_HEADS = 4

V7X_VMEM_LIMIT_BYTES = 56 * 1024 * 1024
LANES = 128

PROJ_TILE_GATED = (1024, 512)
PROJ_TILE_QKV = (2048, 768)
PROJ_TILE_WIDE = (1024, 1024)
ROW_TILE = 256
NORM_ROWS = 1024


def _params(n_grid_dims):
    return pltpu.CompilerParams(
        dimension_semantics=("arbitrary",) * n_grid_dims,
        vmem_limit_bytes=V7X_VMEM_LIMIT_BYTES)


def _resident(shape):
    return pl.BlockSpec(shape, lambda *_: (0,) * len(shape), pipeline_mode=pl.Buffered(1))


def _rms(y, g):
    ms = jnp.mean(y * y, axis=-1, keepdims=True)
    return y * lax.rsqrt(ms + EPS) * g


WEIGHT_CHUNK_ROWS = 256
_HBM = pl.BlockSpec(memory_space=pl.ANY)


def _stage_scratch(n_cols):
    return [pltpu.VMEM((2, WEIGHT_CHUNK_ROWS, n_cols), F32), pltpu.SemaphoreType.DMA((2,))]


def _load_weight_bf16(w_hbm, layer, wb_ref, stage_ref, sem):
    R = WEIGHT_CHUNK_ROWS
    n_chunks = wb_ref.shape[0] // R
    assert wb_ref.shape[0] % R == 0

    def copy(c):
        return pltpu.make_async_copy(w_hbm.at[layer, pl.ds(c * R, R), :], stage_ref.at[c % 2], sem.at[c % 2])

    copy(0).start()
    for c in range(n_chunks):
        if c + 1 < n_chunks:
            copy(c + 1).start()
        copy(c).wait()
        wb_ref[c * R:(c + 1) * R, :] = stage_ref[c % 2].astype(BF16)


def _norm_kernel(x_ref, g_ref, o_ref, *, bm, chunk=64):
    def body(i, carry):
        r = pl.multiple_of(i * chunk, chunk)
        o_ref[pl.ds(r, chunk), :] = _rms(x_ref[pl.ds(r, chunk), :], g_ref[...]).astype(BF16)
        return carry
    lax.fori_loop(0, bm // chunk, body, 0)


def rms_norm_bf16(x, g):
    M, K = x.shape
    bm = NORM_ROWS
    assert M % bm == 0
    return pl.pallas_call(
        functools.partial(_norm_kernel, bm=bm),
        grid=(M // bm,),
        in_specs=[pl.BlockSpec((bm, K), lambda i: (i, 0)), pl.BlockSpec((1, K), lambda i: (0, 0))],
        out_specs=pl.BlockSpec((bm, K), lambda i: (i, 0)),
        out_shape=jax.ShapeDtypeStruct((M, K), BF16),
        compiler_params=_params(1),
        name="rms_norm",
    )(x, g.reshape(1, K))


def _weight_spec(w_stack, layer, col0, bn):
    K = w_stack.shape[1]
    assert col0 % LANES == 0 and bn % LANES == 0
    return pl.BlockSpec((pl.Element(1), pl.Element(K), pl.Element(bn)),
                        lambda j, i: (layer, 0, pl.multiple_of(col0 + j * bn, LANES)))


def _proj_kernel(*refs, act):
    if act == "colscale":
        h_ref, w_ref, cs_ref, o_ref, wb_ref = refs
    else:
        h_ref, w_ref, o_ref, wb_ref = refs

    @pl.when(pl.program_id(1) == 0)
    def _():
        wb_ref[...] = w_ref[0].astype(BF16)

    acc = jnp.dot(h_ref[...], wb_ref[...], preferred_element_type=F32)
    if act == "sigmoid":
        acc = jax.nn.sigmoid(acc)
    elif act == "colscale":
        acc = acc * cs_ref[...]
    o_ref[...] = acc.astype(o_ref.dtype)


def project(h, w_stack, layer, col0, n_cols, tile, *, act="none", colscale=None):
    M, K = h.shape
    bm, bn = tile
    assert M % bm == 0 and n_cols % bn == 0 and (act == "colscale") == (colscale is not None)
    in_specs = [pl.BlockSpec((bm, K), lambda j, i: (i, 0)), _weight_spec(w_stack, layer, col0, bn)]
    args = [h, w_stack]
    if colscale is not None:
        in_specs.append(pl.BlockSpec((1, bn), lambda j, i: (0, j)))
        args.append(colscale.reshape(1, n_cols))
    return pl.pallas_call(
        functools.partial(_proj_kernel, act=act),
        grid=(n_cols // bn, M // bm),
        in_specs=in_specs,
        out_specs=pl.BlockSpec((bm, bn), lambda j, i: (i, j)),
        out_shape=jax.ShapeDtypeStruct((M, n_cols), BF16),
        scratch_shapes=[pltpu.VMEM((K, bn), BF16)],
        compiler_params=_params(2),
        name="project_" + act,
    )(*args)


def _proj_gated_kernel(h_ref, wa_ref, wb_ref, o_ref, wa_bf, wb_bf, *, mode):
    @pl.when(pl.program_id(1) == 0)
    def _():
        wa_bf[...] = wa_ref[0].astype(BF16)
        wb_bf[...] = wb_ref[0].astype(BF16)

    h = h_ref[...]
    a = jnp.dot(h, wa_bf[...], preferred_element_type=F32)
    b = jnp.dot(h, wb_bf[...], preferred_element_type=F32)
    if mode == "glu":
        y = a * jax.nn.sigmoid(b)
    else:
        y = a * jax.nn.sigmoid(a) * b
    o_ref[...] = y.astype(o_ref.dtype)


def project_gated(h, wa, wb, n_cols, *, mode):
    M, K = h.shape
    bm, bn = PROJ_TILE_GATED
    assert M % bm == 0 and n_cols % bn == 0
    return pl.pallas_call(
        functools.partial(_proj_gated_kernel, mode=mode),
        grid=(n_cols // bn, M // bm),
        in_specs=[pl.BlockSpec((bm, K), lambda j, i: (i, 0)), _weight_spec(*wa, bn), _weight_spec(*wb, bn)],
        out_specs=pl.BlockSpec((bm, bn), lambda j, i: (i, j)),
        out_shape=jax.ShapeDtypeStruct((M, n_cols), BF16),
        scratch_shapes=[pltpu.VMEM((K, bn), BF16), pltpu.VMEM((K, bn), BF16)],
        compiler_params=_params(2),
        name="project_" + mode,
    )(h, wa[0], wb[0])


MEM_KV_COLS = 256


def _mem_kv_kernel(mem_ref, g_ref, wk_ref, wv_ref, k_ref, v_ref, memn_ref, *, rows, chunk=64):
    @pl.when(pl.program_id(0) == 0)
    def _():
        def body(i, carry):
            r = pl.multiple_of(i * chunk, chunk)
            memn_ref[pl.ds(r, chunk), :] = _rms(mem_ref[pl.ds(r, chunk), :], g_ref[...]).astype(BF16)
            return carry
        lax.fori_loop(0, rows // chunk, body, 0)

    memn = memn_ref[...]
    k_ref[...] = jnp.dot(memn, wk_ref[0].astype(BF16), preferred_element_type=F32).astype(BF16)
    v_ref[...] = jnp.dot(memn, wv_ref[0].astype(BF16), preferred_element_type=F32).astype(BF16)


def memory_kv(mem, g, wk_stack, wv_stack, layer):
    R, K = mem.shape
    N = wk_stack.shape[2]
    bn = MEM_KV_COLS
    assert N % bn == 0 and R % 64 == 0
    w_spec = pl.BlockSpec((pl.Element(1), pl.Element(K), pl.Element(bn)),
                          lambda j: (layer, 0, pl.multiple_of(j * bn, LANES)))
    out_spec = pl.BlockSpec((R, bn), lambda j: (0, j))
    out = jax.ShapeDtypeStruct((R, N), BF16)
    return pl.pallas_call(
        functools.partial(_mem_kv_kernel, rows=R),
        grid=(N // bn,),
        in_specs=[_resident((R, K)), _resident((1, K)), w_spec, w_spec],
        out_specs=(out_spec, out_spec),
        out_shape=(out, out),
        scratch_shapes=[pltpu.VMEM((R, K), BF16)],
        compiler_params=_params(1),
        name="memory_kv",
    )(mem, g.reshape(1, K), wk_stack, wv_stack)


def _down_kernel(*refs, layer, emit_next):
    if emit_next:
        a_ref, w_hbm, x_ref, g_ref, gn_ref, o_ref, hn_ref, w_ref, stage_ref, sem = refs
    else:
        a_ref, w_hbm, x_ref, g_ref, o_ref, w_ref, stage_ref, sem = refs

    @pl.when(pl.program_id(0) == 0)
    def _():
        _load_weight_bf16(w_hbm, layer, w_ref, stage_ref, sem)

    y = jnp.dot(a_ref[...], w_ref[...], preferred_element_type=F32)
    xn = x_ref[...] + _rms(y, g_ref[...])
    o_ref[...] = xn
    if emit_next:
        hn_ref[...] = _rms(xn, gn_ref[...]).astype(BF16)


def down_residual(a, w_stack, layer, x, g, g_next):
    M, K = a.shape
    N = w_stack.shape[2]
    bm = ROW_TILE
    emit_next = g_next is not None
    assert M % bm == 0
    tile = pl.BlockSpec((bm, N), lambda i: (i, 0))
    in_specs = [pl.BlockSpec((bm, K), lambda i: (i, 0)), _HBM, tile, _resident((1, N))]
    args = [a, w_stack, x, g.reshape(1, N)]
    out_specs, out_shape = tile, jax.ShapeDtypeStruct((M, N), F32)
    if emit_next:
        in_specs.append(_resident((1, N)))
        args.append(g_next.reshape(1, N))
        out_specs, out_shape = (tile, tile), (out_shape, jax.ShapeDtypeStruct((M, N), BF16))
    res = pl.pallas_call(
        functools.partial(_down_kernel, layer=layer, emit_next=emit_next),
        grid=(M // bm,),
        in_specs=in_specs,
        out_specs=out_specs,
        out_shape=out_shape,
        scratch_shapes=[pltpu.VMEM((K, N), BF16)] + _stage_scratch(N),
        compiler_params=_params(1),
        name="down_residual",
    )(*args)
    return res if emit_next else (res, None)


CONV_ROWS = 128
CONV_LN_ROWS = 8
CONV_LN_INTERLEAVE = 16
CONV_DATA_OFF = 16


def _conv_kernel(h_ref, wdw_ref, bdw_ref, lng_ref, lnb_ref, o_ref, hp_ref, cv_ref, *, S, C, KW):
    pad = KW // 2
    off = CONV_DATA_OFF
    n_slab = C // LANES
    R = CONV_ROWS

    for j in range(n_slab):
        cols = slice(j * LANES, (j + 1) * LANES)
        hp_ref[j, 0:off, :] = jnp.zeros((off, LANES), F32)
        hp_ref[j, off + S:off + S + off, :] = jnp.zeros((off, LANES), F32)

        def copy(i, carry, j=j, cols=cols):
            r = pl.multiple_of(i * 256, 256)
            hp_ref[j, pl.ds(off + r, 256), :] = h_ref[pl.ds(r, 256), cols].astype(F32)
            return carry
        lax.fori_loop(0, S // 256, copy, 0)

        def block(i, carry, j=j, cols=cols):
            for phase in range(2):
                t0 = i * (2 * R) + phase
                acc = jnp.broadcast_to(bdw_ref[:, cols], (R, LANES))
                for k in range(KW):
                    taps = hp_ref[j, pl.ds(t0 + (off - pad + k), R, stride=2), :]
                    acc = acc + taps * wdw_ref[k:k + 1, cols]
                cv_ref[j, pl.ds(t0, R, stride=2), :] = acc
            return carry
        lax.fori_loop(0, S // (2 * R), block, 0)

    def norm(i, carry):
        for u in range(CONV_LN_INTERLEAVE):
            start = (i * CONV_LN_INTERLEAVE + u) * CONV_LN_ROWS
            rows = pl.ds(pl.multiple_of(start, CONV_LN_ROWS), CONV_LN_ROWS)
            xs = [cv_ref[j, rows, :] for j in range(n_slab)]
            mu = jnp.sum(sum(xs), axis=-1, keepdims=True) * (1.0 / C)
            xs = [x - mu for x in xs]
            var = jnp.sum(sum(x * x for x in xs), axis=-1, keepdims=True) * (1.0 / C)
            inv = lax.rsqrt(var + EPS)
            for j in range(n_slab):
                cols = slice(j * LANES, (j + 1) * LANES)
                y = xs[j] * inv * lng_ref[:, cols] + lnb_ref[:, cols]
                o_ref[rows, cols] = (y * jax.nn.sigmoid(y)).astype(o_ref.dtype)
        return carry
    lax.fori_loop(0, S // (CONV_LN_ROWS * CONV_LN_INTERLEAVE), norm, 0)


def conv_branch(h, w_dw, b_dw, ln_g, ln_b):
    B, S, C = h.shape
    KW = w_dw.shape[0]
    assert KW // 2 < CONV_DATA_OFF and S % (2 * CONV_ROWS) == 0 and C % LANES == 0
    row = lambda v: v.reshape(1, C)
    vec = pl.BlockSpec((1, C), lambda b: (0, 0))
    return pl.pallas_call(
        functools.partial(_conv_kernel, S=S, C=C, KW=KW),
        grid=(B,),
        in_specs=[
            pl.BlockSpec((None, S, C), lambda b: (b, 0, 0)),
            pl.BlockSpec((KW, C), lambda b: (0, 0)),
            vec, vec, vec,
        ],
        out_specs=pl.BlockSpec((None, S, C), lambda b: (b, 0, 0)),
        out_shape=jax.ShapeDtypeStruct((B, S, C), BF16),
        scratch_shapes=[pltpu.VMEM((C // LANES, S + 2 * CONV_DATA_OFF, LANES), F32),
                        pltpu.VMEM((C // LANES, S, LANES), F32)],
        compiler_params=_params(1),
        name="conv_branch",
    )(h, w_dw, row(b_dw), row(ln_g), row(ln_b))


ATTN_Q_ROWS = 128
ATTN_SPLIT_STRIDE = 4


def _attn_kernel(q0, k0, v0, q1, k1, v1, q2, k2, v2, o_ref,
                 qf, kf, vf, qt, kt, vt, qs, kp, vp, og0, og1, og2, lg0, lg1, lg2, bias_ref, s_buf, p_buf, rd_buf, *, S):
    T = ATTN_Q_ROWS
    HB = BAND_HALF
    W = T + 2 * HB
    D = HEAD_DIM
    n_blocks = S // T
    qkv = ((q0, k0, v0), (q1, k1, v1), (q2, k2, v2))
    outs = ((og0, lg0), (og1, lg1), (og2, lg2))
    slot_f = jnp.full((T, W), pl.program_id(1), jnp.int32).astype(F32)

    a_idx = lax.broadcasted_iota(jnp.int32, (T, W), 0)
    c_idx = lax.broadcasted_iota(jnp.int32, (T, W), 1)
    c_row = lax.broadcasted_iota(jnp.int32, (1, W), 1)

    for g, (_, d) in enumerate(DIL_GROUPS):
        L = S // d
        nb = L // T
        pad = HB if nb > 1 else 0
        P = L + 2 * pad
        Wg = T + 2 * pad
        q_ref, k_ref, v_ref = qkv[g]
        og, lg = outs[g]

        slope = jnp.exp2(-8.0 * (slot_f + (HEADS_PER_GROUP * g + 1.0)) / N_ATTN_HEADS)
        rel = jnp.abs(c_idx - pad - a_idx)
        bias = jnp.where(rel <= HB, -(slope * (d * rel).astype(F32)), NEG_INF)
        bias_ref[:, 0:Wg] = bias[:, 0:Wg]

        if pad:
            kp[0:HB, :] = jnp.zeros((HB, D), BF16)
            vp[0:HB, :] = jnp.zeros((HB, D), BF16)

            def clear(r, carry, L=L, P=P):
                r0 = pl.multiple_of(r * P + (HB + L), HB)
                kp[pl.ds(r0, 2 * HB), :] = jnp.zeros((2 * HB, D), BF16)
                vp[pl.ds(r0, 2 * HB), :] = jnp.zeros((2 * HB, D), BF16)
                return carry
            lax.fori_loop(0, d, clear, 0)

        if d == 1:
            def copy(i, carry, k_ref=k_ref, v_ref=v_ref):
                r0 = pl.multiple_of(i * 256, 256)
                kp[pl.ds(HB + r0, 256), :] = k_ref[pl.ds(r0, 256), :]
                vp[pl.ds(HB + r0, 256), :] = v_ref[pl.ds(r0, 256), :]
                return carry
            lax.fori_loop(0, S // 256, copy, 0)
            q_src = q_ref
        else:
            def widen(i, carry, q_ref=q_ref, k_ref=k_ref, v_ref=v_ref):
                r0 = pl.multiple_of(i * 256, 256)
                qf[pl.ds(r0, 256), :] = q_ref[pl.ds(r0, 256), :].astype(F32)
                kf[pl.ds(r0, 256), :] = k_ref[pl.ds(r0, 256), :].astype(F32)
                vf[pl.ds(r0, 256), :] = v_ref[pl.ds(r0, 256), :].astype(F32)
                return carry
            lax.fori_loop(0, S // 256, widen, 0)

            if d % (ATTN_SPLIT_STRIDE ** 2) == 0:
                s1 = ATTN_SPLIT_STRIDE
                s2 = d // s1
                part = S // s1
                for src_ref, tmp_ref in ((qf, qt), (kf, kt), (vf, vt)):
                    for r1 in range(s1):
                        tmp_ref[r1 * part:(r1 + 1) * part, :] = src_ref[pl.ds(r1, part, stride=s1), :]
                strided = (qt, kt, vt)
                start = lambda r, c: (r % s1) * part + r // s1 + s2 * c * T
                stride = s2
            else:
                strided = (qf, kf, vf)
                start = lambda r, c: r + d * c * T
                stride = d

            def gather(r, carry, L=L, P=P, pad=pad, strided=strided, start=start, stride=stride):
                for c in range(L // T):
                    src = pl.ds(start(r, c), T, stride=stride)
                    qs[pl.ds(pl.multiple_of(r * L + c * T, T), T), :] = strided[0][src, :].astype(BF16)
                    dst = pl.ds(pl.multiple_of(r * P + (pad + c * T), HB), T)
                    kp[dst, :] = strided[1][src, :].astype(BF16)
                    vp[dst, :] = strided[2][src, :].astype(BF16)
                return carry
            lax.fori_loop(0, d, gather, 0, unroll=2)
            q_src = qs

        def geometry(n):
            r, i = divmod(n, nb)
            win = slice(r * P + i * T, r * P + i * T + Wg)
            dst = slice(n * T, (n + 1) * T) if d == 1 else pl.ds(r + d * i * T, T, stride=d)
            return i, win, dst

        for n in range(n_blocks):
            i, win, _ = geometry(n)
            s = lax.dot_general(q_src[n * T:(n + 1) * T, :], kp[win, :], (((1,), (1,)), ((), ())),
                                preferred_element_type=F32)
            s = s + bias_ref[:, 0:Wg]
            kidx = c_row + (i * T - HB)
            if pad and i == 0:
                s = s + jnp.where(kidx >= 0, 0.0, NEG_INF)
            if pad and i == nb - 1:
                s = s + jnp.where(kidx < L, 0.0, NEG_INF)
            s_buf[n, :, 0:Wg] = s
        for n in range(n_blocks):
            _, _, dst = geometry(n)
            m = jnp.max(s_buf[n, :, 0:Wg], axis=-1, keepdims=True)
            p = jnp.exp(s_buf[n, :, 0:Wg] - m)
            den = jnp.sum(p, axis=-1, keepdims=True)
            p_buf[n, :, 0:Wg] = p.astype(BF16)
            rd_buf[n] = jnp.broadcast_to(1.0 / den, (T, D))
            lg[dst, :] = jnp.broadcast_to(m + jnp.log(den), (T, D))
        for n in range(n_blocks):
            _, win, dst = geometry(n)
            og[dst, :] = jnp.dot(p_buf[n, :, 0:Wg], vp[win, :], preferred_element_type=F32) * rd_buf[n]

    def mix(i, carry):
        r0 = pl.multiple_of(i * 256, 256)
        rows = pl.ds(r0, 256)
        l0, l1, l2 = lg0[rows, :], lg1[rows, :], lg2[rows, :]
        m = jnp.maximum(jnp.maximum(l0, l1), l2)
        w0, w1, w2 = jnp.exp(l0 - m), jnp.exp(l1 - m), jnp.exp(l2 - m)
        y = (w0 * og0[rows, :] + w1 * og1[rows, :] + w2 * og2[rows, :]) / (w0 + w1 + w2)
        o_ref[rows, :] = y.astype(o_ref.dtype)
        return carry
    lax.fori_loop(0, S // 256, mix, 0)


def dilated_mixture_attention(qkv):
    B, S, _ = qkv.shape
    D = HEAD_DIM
    T = ATTN_Q_ROWS
    assert all(S % (d * T) == 0 and w // (2 * d) == BAND_HALF for w, d in DIL_GROUPS)
    assert S % 256 == 0
    max_d = max(d for _, d in DIL_GROUPS)

    def head_spec(part, g):
        return pl.BlockSpec((None, S, D),
                            lambda b, j: (b, 0, part * N_ATTN_HEADS + HEADS_PER_GROUP * g + j))
    in_specs = [head_spec(part, g) for g in range(len(DIL_GROUPS)) for part in range(3)]
    seq_f32 = pltpu.VMEM((S, D), F32)
    return pl.pallas_call(
        functools.partial(_attn_kernel, S=S),
        grid=(B, HEADS_PER_GROUP),
        in_specs=in_specs,
        out_specs=pl.BlockSpec((None, S, D), lambda b, j: (b, 0, j)),
        out_shape=jax.ShapeDtypeStruct((B, S, HEADS_PER_GROUP * D), BF16),
        scratch_shapes=[
            seq_f32, seq_f32, seq_f32,
            seq_f32, seq_f32, seq_f32,
            pltpu.VMEM((S, D), BF16),
            pltpu.VMEM((S + 2 * BAND_HALF * max_d + BAND_HALF, D), BF16),
            pltpu.VMEM((S + 2 * BAND_HALF * max_d + BAND_HALF, D), BF16),
            seq_f32, seq_f32, seq_f32,
            seq_f32, seq_f32, seq_f32,
            pltpu.VMEM((T, T + 2 * BAND_HALF), F32),
            pltpu.VMEM((S // T, T, T + 2 * BAND_HALF), F32),
            pltpu.VMEM((S // T, T, T + 2 * BAND_HALF), BF16),
            pltpu.VMEM((S // T, T, D), F32),
        ],
        compiler_params=_params(2),
        name="dilated_attention",
    )(*([qkv] * 9))


def _merge_kernel(c_ref, a_ref, ga_ref, gb_ref, x_ref, wc_hbm, wa_hbm, wo_hbm, g_ref, o_ref,
                  wc_ref, wa_ref, wo_ref, stage_ref, sem, *, layer):
    @pl.when(pl.program_id(0) == 0)
    def _():
        _load_weight_bf16(wc_hbm, layer, wc_ref, stage_ref, sem)
        _load_weight_bf16(wa_hbm, layer, wa_ref, stage_ref, sem)
        _load_weight_bf16(wo_hbm, layer, wo_ref, stage_ref, sem)

    yc = jnp.dot(c_ref[...], wc_ref[...], preferred_element_type=F32)
    ya = jnp.dot(a_ref[...], wa_ref[...], preferred_element_type=F32)
    merged = ga_ref[...].astype(F32) * yc + gb_ref[...].astype(F32) * ya
    y = jnp.dot(merged.astype(BF16), wo_ref[...], preferred_element_type=F32)
    o_ref[...] = x_ref[...] + _rms(y, g_ref[...])


def merge_out(c, a, gates, x, wc, wa, wo, layer, g):
    M, D = x.shape
    Cc, Ca = c.shape[1], a.shape[1]
    bm = ROW_TILE
    assert M % bm == 0
    return pl.pallas_call(
        functools.partial(_merge_kernel, layer=layer),
        grid=(M // bm,),
        in_specs=[
            pl.BlockSpec((bm, Cc), lambda i: (i, 0)),
            pl.BlockSpec((bm, Ca), lambda i: (i, 0)),
            pl.BlockSpec((bm, D), lambda i: (i, 0)),
            pl.BlockSpec((bm, D), lambda i: (i, 1)),
            pl.BlockSpec((bm, D), lambda i: (i, 0)),
            _HBM, _HBM, _HBM, _resident((1, D)),
        ],
        out_specs=pl.BlockSpec((bm, D), lambda i: (i, 0)),
        out_shape=jax.ShapeDtypeStruct((M, D), F32),
        scratch_shapes=[pltpu.VMEM((Cc, D), BF16), pltpu.VMEM((Ca, D), BF16), pltpu.VMEM((D, D), BF16)]
        + _stage_scratch(D),
        compiler_params=_params(1),
        name="merge_out",
    )(c, a, gates, gates, x, wc, wa, wo, g.reshape(1, D))


def _cross_kernel(x_ref, k_ref, v_ref, wq_hbm, wo_hbm, gpre_ref, gpost_ref, gn_ref, o_ref, hn_ref,
                  wq_ref, wo_ref, stage_ref, sem, *, D, layer):
    dh = D // X_HEADS

    @pl.when((pl.program_id(0) == 0) & (pl.program_id(1) == 0))
    def _():
        _load_weight_bf16(wq_hbm, layer, wq_ref, stage_ref, sem)
        _load_weight_bf16(wo_hbm, layer, wo_ref, stage_ref, sem)

    heads = [slice(hd * dh, (hd + 1) * dh) for hd in range(X_HEADS)]
    h = _rms(x_ref[...], gpre_ref[...]).astype(BF16)
    q = (jnp.dot(h, wq_ref[...], preferred_element_type=F32) * (dh ** -0.5)).astype(BF16)
    scores = [lax.dot_general(q[:, cols], k_ref[:, cols], (((1,), (1,)), ((), ())),
                              preferred_element_type=F32) for cols in heads]
    probs = []
    for s in scores:
        p = jnp.exp(s - jnp.max(s, axis=-1, keepdims=True))
        probs.append((p * (1.0 / jnp.sum(p, axis=-1, keepdims=True))).astype(BF16))
    ctx = [jnp.dot(p, v_ref[:, cols], preferred_element_type=F32).astype(BF16)
           for p, cols in zip(probs, heads)]
    y = jnp.dot(jnp.concatenate(ctx, axis=-1), wo_ref[...], preferred_element_type=F32)
    xn = x_ref[...] + _rms(y, gpost_ref[...])
    o_ref[...] = xn
    hn_ref[...] = _rms(xn, gn_ref[...]).astype(BF16)


def cross_attention_sublayer(x, kmem, vmem, wq, wo, layer, g_pre, g_post, g_next):
    B, S, D = x.shape
    Nm = kmem.shape[1]
    bm = ROW_TILE
    assert S % bm == 0
    tile = pl.BlockSpec((None, bm, D), lambda b, i: (b, i, 0))
    mem_spec = pl.BlockSpec((None, Nm, D), lambda b, i: (b, 0, 0))
    vec = lambda g: g.reshape(1, D)
    return pl.pallas_call(
        functools.partial(_cross_kernel, D=D, layer=layer),
        grid=(B, S // bm),
        in_specs=[tile, mem_spec, mem_spec, _HBM, _HBM,
                  _resident((1, D)), _resident((1, D)), _resident((1, D))],
        out_specs=(tile, tile),
        out_shape=(jax.ShapeDtypeStruct((B, S, D), F32), jax.ShapeDtypeStruct((B, S, D), BF16)),
        scratch_shapes=[pltpu.VMEM((D, D), BF16), pltpu.VMEM((D, D), BF16)] + _stage_scratch(D),
        compiler_params=_params(2),
        name="cross_attention",
    )(x, kmem, vmem, wq, wo, vec(g_pre), vec(g_post), vec(g_next))


def kernel(x, mem, w_in, w_dw, b_dw, conv_ln_g, conv_ln_b, w_conv_out, w_attn_proj, w_o, g_mix_pre, g_mix_post, g_mem, w_cq, w_ck, w_cv, w_co, g_x_pre, g_x_post, w_ffn_gate, w_ffn_up, w_ffn_down, g_ffn_pre, g_ffn_post):
    B, S, D = x.shape
    depth = w_in.shape[0]
    Cc = w_dw.shape[-1]
    Wq = N_ATTN_HEADS * HEAD_DIM
    Nm = mem.shape[1]
    d_ff = w_ffn_gate.shape[-1]
    c0 = 2 * Cc
    c3 = c0 + 3 * Wq
    M = B * S

    q_scale = jnp.concatenate([jnp.full((Wq,), HEAD_DIM ** -0.5, F32), jnp.ones((2 * Wq,), F32)])
    mem2 = mem.reshape(B * Nm, D)
    x2 = x.reshape(M, D)
    h = rms_norm_bf16(x2, g_mix_pre[0])
    for l in range(depth):
        glu = project_gated(h, (w_in, l, 0), (w_in, l, Cc), Cc, mode="glu")
        qkv = project(h, w_in, l, c0, 3 * Wq, PROJ_TILE_QKV, act="colscale", colscale=q_scale)
        gates = project(h, w_in, l, c3, 2 * D, PROJ_TILE_WIDE, act="sigmoid")
        conv = conv_branch(glu.reshape(B, S, Cc), w_dw[l], b_dw[l], conv_ln_g[l], conv_ln_b[l])
        attn = dilated_mixture_attention(qkv.reshape(B, S, 3 * Wq))
        x2 = merge_out(conv.reshape(M, Cc), attn.reshape(M, -1), gates, x2,
                       w_conv_out, w_attn_proj, w_o, l, g_mix_post[l])
        kmem, vmem = memory_kv(mem2, g_mem[l], w_ck, w_cv, l)
        x3, h3 = cross_attention_sublayer(x2.reshape(B, S, D), kmem.reshape(B, Nm, D), vmem.reshape(B, Nm, D),
                                          w_cq, w_co, l, g_x_pre[l], g_x_post[l], g_ffn_pre[l])
        x2, h = x3.reshape(M, D), h3.reshape(M, D)
        act = project_gated(h, (w_ffn_gate, l, 0), (w_ffn_up, l, 0), d_ff, mode="swiglu")
        x2, h = down_residual(act, w_ffn_down, l, x2, g_ffn_post[l],
                              g_mix_pre[l + 1] if l + 1 < depth else None)
    return x2.reshape(B, S, D)
```

```python
import functools

import jax
import jax.numpy as jnp
from jax import lax
from jax.experimental import pallas as pl
from jax.experimental.pallas import tpu as pltpu

F32 = jnp.float32
BF16 = jnp.bfloat16

EPS = 1e-6
NEG_INF = -1e30

HEAD_DIM = 128
DIL_GROUPS = ((128, 1), (512, 4), (2048, 16))
HEADS_PER_GROUP = 4
N_ATTN_HEADS = HEADS_PER_GROUP * len(DIL_GROUPS)
BAND_HALF = 64
X_HEADS = 4

V7X_VMEM_LIMIT_BYTES = 56 * 1024 * 1024
LANES = 128

PROJ_TILE_GATED = (1024, 512)
PROJ_TILE_QKV = (1024, 1536)
PROJ_TILE_WIDE = (1024, 1024)
ROW_TILE = 256
NORM_ROWS = 1024
NORM_CHUNK = 64
COPY_ROWS = 256


def _params(n_grid_dims):
    return pltpu.CompilerParams(
        dimension_semantics=("arbitrary",) * n_grid_dims,
        vmem_limit_bytes=V7X_VMEM_LIMIT_BYTES)


def _resident(shape):
    return pl.BlockSpec(shape, lambda *_: (0,) * len(shape), pipeline_mode=pl.Buffered(1))


def _rms(y, g):
    ms = jnp.mean(y * y, axis=-1, keepdims=True)
    return y * lax.rsqrt(ms + EPS) * g


WEIGHT_CHUNK_ROWS = 256
_HBM = pl.BlockSpec(memory_space=pl.ANY)


def _stage_scratch(n_cols):
    return [pltpu.VMEM((2, WEIGHT_CHUNK_ROWS, n_cols), F32), pltpu.SemaphoreType.DMA((2,))]


def _load_weight_bf16(w_hbm, layer, wb_ref, stage_ref, sem):
    R = WEIGHT_CHUNK_ROWS
    n_chunks = wb_ref.shape[0] // R
    assert wb_ref.shape[0] % R == 0

    def copy(c):
        return pltpu.make_async_copy(w_hbm.at[layer, pl.ds(c * R, R), :], stage_ref.at[c % 2], sem.at[c % 2])

    copy(0).start()
    for c in range(n_chunks):
        if c + 1 < n_chunks:
            copy(c + 1).start()
        copy(c).wait()
        wb_ref[c * R:(c + 1) * R, :] = stage_ref[c % 2].astype(BF16)


def _rms_rows_to_bf16(x_ref, g_ref, o_ref, rows):
    def body(i, carry):
        r = pl.multiple_of(i * NORM_CHUNK, NORM_CHUNK)
        o_ref[pl.ds(r, NORM_CHUNK), :] = _rms(x_ref[pl.ds(r, NORM_CHUNK), :], g_ref[...]).astype(BF16)
        return carry
    lax.fori_loop(0, rows // NORM_CHUNK, body, 0)


def _norm_kernel(x_ref, g_ref, o_ref, *, bm):
    _rms_rows_to_bf16(x_ref, g_ref, o_ref, bm)


def rms_norm_bf16(x, g):
    M, K = x.shape
    bm = NORM_ROWS
    assert M % bm == 0
    return pl.pallas_call(
        functools.partial(_norm_kernel, bm=bm),
        grid=(M // bm,),
        in_specs=[pl.BlockSpec((bm, K), lambda i: (i, 0)), pl.BlockSpec((1, K), lambda i: (0, 0))],
        out_specs=pl.BlockSpec((bm, K), lambda i: (i, 0)),
        out_shape=jax.ShapeDtypeStruct((M, K), BF16),
        compiler_params=_params(1),
        name="rms_norm",
    )(x, g.reshape(1, K))


def _weight_spec(w_stack, layer, col0, bn):
    K = w_stack.shape[1]
    assert col0 % LANES == 0 and bn % LANES == 0
    return pl.BlockSpec((pl.Element(1), pl.Element(K), pl.Element(bn)),
                        lambda j, i: (layer, 0, pl.multiple_of(col0 + j * bn, LANES)))


def _proj_kernel(*refs, act):
    if act == "colscale":
        h_ref, w_ref, cs_ref, o_ref, wb_ref = refs
    else:
        h_ref, w_ref, o_ref, wb_ref = refs

    @pl.when(pl.program_id(1) == 0)
    def _():
        wb_ref[...] = w_ref[0].astype(BF16)

    acc = jnp.dot(h_ref[...], wb_ref[...], preferred_element_type=F32)
    if act == "sigmoid":
        acc = jax.nn.sigmoid(acc)
    elif act == "colscale":
        acc = acc * cs_ref[...]
    o_ref[...] = acc.astype(o_ref.dtype)


def project(h, w_stack, layer, col0, n_cols, tile, *, act, colscale=None):
    M, K = h.shape
    bm, bn = tile
    assert M % bm == 0 and n_cols % bn == 0 and (act == "colscale") == (colscale is not None)
    in_specs = [pl.BlockSpec((bm, K), lambda j, i: (i, 0)), _weight_spec(w_stack, layer, col0, bn)]
    args = [h, w_stack]
    if colscale is not None:
        in_specs.append(pl.BlockSpec((1, bn), lambda j, i: (0, j)))
        args.append(colscale.reshape(1, n_cols))
    return pl.pallas_call(
        functools.partial(_proj_kernel, act=act),
        grid=(n_cols // bn, M // bm),
        in_specs=in_specs,
        out_specs=pl.BlockSpec((bm, bn), lambda j, i: (i, j)),
        out_shape=jax.ShapeDtypeStruct((M, n_cols), BF16),
        scratch_shapes=[pltpu.VMEM((K, bn), BF16)],
        compiler_params=_params(2),
        name="project_" + act,
    )(*args)


def _proj_gated_kernel(h_ref, wa_ref, wb_ref, o_ref, wa_bf, wb_bf, *, mode):
    @pl.when(pl.program_id(1) == 0)
    def _():
        wa_bf[...] = wa_ref[0].astype(BF16)
        wb_bf[...] = wb_ref[0].astype(BF16)

    h = h_ref[...]
    a = jnp.dot(h, wa_bf[...], preferred_element_type=F32)
    b = jnp.dot(h, wb_bf[...], preferred_element_type=F32)
    if mode == "glu":
        y = a * jax.nn.sigmoid(b)
    else:
        y = a * jax.nn.sigmoid(a) * b
    o_ref[...] = y.astype(o_ref.dtype)


def project_gated(h, wa, wb, n_cols, *, mode):
    M, K = h.shape
    bm, bn = PROJ_TILE_GATED
    assert M % bm == 0 and n_cols % bn == 0
    return pl.pallas_call(
        functools.partial(_proj_gated_kernel, mode=mode),
        grid=(n_cols // bn, M // bm),
        in_specs=[pl.BlockSpec((bm, K), lambda j, i: (i, 0)), _weight_spec(*wa, bn), _weight_spec(*wb, bn)],
        out_specs=pl.BlockSpec((bm, bn), lambda j, i: (i, j)),
        out_shape=jax.ShapeDtypeStruct((M, n_cols), BF16),
        scratch_shapes=[pltpu.VMEM((K, bn), BF16), pltpu.VMEM((K, bn), BF16)],
        compiler_params=_params(2),
        name="project_" + mode,
    )(h, wa[0], wb[0])


MEM_KV_COLS = 256


def _mem_kv_kernel(mem_ref, g_ref, wk_ref, wv_ref, k_ref, v_ref, memn_ref, *, rows):
    @pl.when(pl.program_id(0) == 0)
    def _():
        _rms_rows_to_bf16(mem_ref, g_ref, memn_ref, rows)

    memn = memn_ref[...]
    k_ref[...] = jnp.dot(memn, wk_ref[0].astype(BF16), preferred_element_type=F32).astype(BF16)
    v_ref[...] = jnp.dot(memn, wv_ref[0].astype(BF16), preferred_element_type=F32).astype(BF16)


def memory_kv(mem, g, wk_stack, wv_stack, layer):
    R, K = mem.shape
    N = wk_stack.shape[2]
    bn = MEM_KV_COLS
    assert N % bn == 0 and R % NORM_CHUNK == 0
    w_spec = pl.BlockSpec((pl.Element(1), pl.Element(K), pl.Element(bn)),
                          lambda j: (layer, 0, pl.multiple_of(j * bn, LANES)))
    out_spec = pl.BlockSpec((R, bn), lambda j: (0, j))
    out = jax.ShapeDtypeStruct((R, N), BF16)
    return pl.pallas_call(
        functools.partial(_mem_kv_kernel, rows=R),
        grid=(N // bn,),
        in_specs=[_resident((R, K)), _resident((1, K)), w_spec, w_spec],
        out_specs=(out_spec, out_spec),
        out_shape=(out, out),
        scratch_shapes=[pltpu.VMEM((R, K), BF16)],
        compiler_params=_params(1),
        name="memory_kv",
    )(mem, g.reshape(1, K), wk_stack, wv_stack)


def _down_kernel(*refs, layer, emit_next):
    if emit_next:
        a_ref, w_hbm, x_ref, g_ref, gn_ref, o_ref, hn_ref, w_ref, stage_ref, sem = refs
    else:
        a_ref, w_hbm, x_ref, g_ref, o_ref, w_ref, stage_ref, sem = refs

    @pl.when(pl.program_id(0) == 0)
    def _():
        _load_weight_bf16(w_hbm, layer, w_ref, stage_ref, sem)

    y = jnp.dot(a_ref[...], w_ref[...], preferred_element_type=F32)
    xn = x_ref[...] + _rms(y, g_ref[...])
    o_ref[...] = xn
    if emit_next:
        hn_ref[...] = _rms(xn, gn_ref[...]).astype(BF16)


def down_residual(a, w_stack, layer, x, g, g_next):
    M, K = a.shape
    N = w_stack.shape[2]
    bm = ROW_TILE
    emit_next = g_next is not None
    assert M % bm == 0
    tile = pl.BlockSpec((bm, N), lambda i: (i, 0))
    in_specs = [pl.BlockSpec((bm, K), lambda i: (i, 0)), _HBM, tile, _resident((1, N))]
    args = [a, w_stack, x, g.reshape(1, N)]
    out_specs, out_shape = tile, jax.ShapeDtypeStruct((M, N), F32)
    if emit_next:
        in_specs.append(_resident((1, N)))
        args.append(g_next.reshape(1, N))
        out_specs, out_shape = (tile, tile), (out_shape, jax.ShapeDtypeStruct((M, N), BF16))
    res = pl.pallas_call(
        functools.partial(_down_kernel, layer=layer, emit_next=emit_next),
        grid=(M // bm,),
        in_specs=in_specs,
        out_specs=out_specs,
        out_shape=out_shape,
        scratch_shapes=[pltpu.VMEM((K, N), BF16)] + _stage_scratch(N),
        compiler_params=_params(1),
        name="down_residual",
    )(*args)
    return res if emit_next else (res, None)


CONV_ROWS = 128
CONV_LN_ROWS = 8
CONV_LN_INTERLEAVE = 16
CONV_DATA_OFF = 16


def _conv_kernel(h_ref, wdw_ref, bdw_ref, lng_ref, lnb_ref, o_ref, hp_ref, cv_ref, *, S, C, KW):
    pad = KW // 2
    off = CONV_DATA_OFF
    n_slab = C // LANES
    R = CONV_ROWS

    for j in range(n_slab):
        cols = slice(j * LANES, (j + 1) * LANES)
        hp_ref[j, 0:off, :] = jnp.zeros((off, LANES), F32)
        hp_ref[j, off + S:off + S + off, :] = jnp.zeros((off, LANES), F32)

        def copy(i, carry, j=j, cols=cols):
            r = pl.multiple_of(i * COPY_ROWS, COPY_ROWS)
            hp_ref[j, pl.ds(off + r, COPY_ROWS), :] = h_ref[pl.ds(r, COPY_ROWS), cols].astype(F32)
            return carry
        lax.fori_loop(0, S // COPY_ROWS, copy, 0)

        def block(i, carry, j=j, cols=cols):
            for phase in range(2):
                t0 = i * (2 * R) + phase
                acc = jnp.broadcast_to(bdw_ref[:, cols], (R, LANES))
                for k in range(KW):
                    taps = hp_ref[j, pl.ds(t0 + (off - pad + k), R, stride=2), :]
                    acc = acc + taps * wdw_ref[k:k + 1, cols]
                cv_ref[j, pl.ds(t0, R, stride=2), :] = acc
            return carry
        lax.fori_loop(0, S // (2 * R), block, 0)

    def norm(i, carry):
        for u in range(CONV_LN_INTERLEAVE):
            start = (i * CONV_LN_INTERLEAVE + u) * CONV_LN_ROWS
            rows = pl.ds(pl.multiple_of(start, CONV_LN_ROWS), CONV_LN_ROWS)
            xs = [cv_ref[j, rows, :] for j in range(n_slab)]
            mu = jnp.sum(sum(xs), axis=-1, keepdims=True) * (1.0 / C)
            xs = [x - mu for x in xs]
            var = jnp.sum(sum(x * x for x in xs), axis=-1, keepdims=True) * (1.0 / C)
            inv = lax.rsqrt(var + EPS)
            for j in range(n_slab):
                cols = slice(j * LANES, (j + 1) * LANES)
                y = xs[j] * inv * lng_ref[:, cols] + lnb_ref[:, cols]
                o_ref[rows, cols] = (y * jax.nn.sigmoid(y)).astype(o_ref.dtype)
        return carry
    lax.fori_loop(0, S // (CONV_LN_ROWS * CONV_LN_INTERLEAVE), norm, 0)


def conv_branch(h, w_dw, b_dw, ln_g, ln_b):
    B, S, C = h.shape
    KW = w_dw.shape[0]
    assert KW // 2 < CONV_DATA_OFF and S % (2 * CONV_ROWS) == 0 and C % LANES == 0
    row = lambda v: v.reshape(1, C)
    vec = pl.BlockSpec((1, C), lambda b: (0, 0))
    return pl.pallas_call(
        functools.partial(_conv_kernel, S=S, C=C, KW=KW),
        grid=(B,),
        in_specs=[
            pl.BlockSpec((None, S, C), lambda b: (b, 0, 0)),
            pl.BlockSpec((KW, C), lambda b: (0, 0)),
            vec, vec, vec,
        ],
        out_specs=pl.BlockSpec((None, S, C), lambda b: (b, 0, 0)),
        out_shape=jax.ShapeDtypeStruct((B, S, C), BF16),
        scratch_shapes=[pltpu.VMEM((C // LANES, S + 2 * CONV_DATA_OFF, LANES), F32),
                        pltpu.VMEM((C // LANES, S, LANES), F32)],
        compiler_params=_params(1),
        name="conv_branch",
    )(h, w_dw, row(b_dw), row(ln_g), row(ln_b))


ATTN_Q_ROWS = 128
ATTN_SPLIT_STRIDE = 4


def _attn_kernel(q0, k0, v0, q1, k1, v1, q2, k2, v2, o_ref,
                 qf, kf, vf, qt, kt, vt, qs, kp, vp, og0, og1, og2, lg0, lg1, lg2, bias_ref, s_buf, p_buf, rd_buf, *, S):
    T = ATTN_Q_ROWS
    HB = BAND_HALF
    W = T + 2 * HB
    D = HEAD_DIM
    n_blocks = S // T
    qkv = ((q0, k0, v0), (q1, k1, v1), (q2, k2, v2))
    outs = ((og0, lg0), (og1, lg1), (og2, lg2))
    slot_f = jnp.full((T, W), pl.program_id(1), jnp.int32).astype(F32)

    a_idx = lax.broadcasted_iota(jnp.int32, (T, W), 0)
    c_idx = lax.broadcasted_iota(jnp.int32, (T, W), 1)
    c_row = lax.broadcasted_iota(jnp.int32, (1, W), 1)

    for g, (_, d) in enumerate(DIL_GROUPS):
        L = S // d
        nb = L // T
        pad = HB if nb > 1 else 0
        P = L + 2 * pad
        Wg = T + 2 * pad
        q_ref, k_ref, v_ref = qkv[g]
        og, lg = outs[g]

        slope = jnp.exp2(-8.0 * (slot_f + (HEADS_PER_GROUP * g + 1.0)) / N_ATTN_HEADS)
        rel = jnp.abs(c_idx - pad - a_idx)
        bias = jnp.where(rel <= HB, -(slope * (d * rel).astype(F32)), NEG_INF)
        bias_ref[:, 0:Wg] = bias[:, 0:Wg]

        if pad:
            kp[0:HB, :] = jnp.zeros((HB, D), BF16)
            vp[0:HB, :] = jnp.zeros((HB, D), BF16)

            def clear(r, carry, L=L, P=P):
                r0 = pl.multiple_of(r * P + (HB + L), HB)
                kp[pl.ds(r0, 2 * HB), :] = jnp.zeros((2 * HB, D), BF16)
                vp[pl.ds(r0, 2 * HB), :] = jnp.zeros((2 * HB, D), BF16)
                return carry
            lax.fori_loop(0, d, clear, 0)

        if d == 1:
            def copy(i, carry, k_ref=k_ref, v_ref=v_ref):
                r0 = pl.multiple_of(i * COPY_ROWS, COPY_ROWS)
                kp[pl.ds(HB + r0, COPY_ROWS), :] = k_ref[pl.ds(r0, COPY_ROWS), :]
                vp[pl.ds(HB + r0, COPY_ROWS), :] = v_ref[pl.ds(r0, COPY_ROWS), :]
                return carry
            lax.fori_loop(0, S // COPY_ROWS, copy, 0)
            q_src = q_ref
        else:
            def widen(i, carry, q_ref=q_ref, k_ref=k_ref, v_ref=v_ref):
                r0 = pl.multiple_of(i * COPY_ROWS, COPY_ROWS)
                qf[pl.ds(r0, COPY_ROWS), :] = q_ref[pl.ds(r0, COPY_ROWS), :].astype(F32)
                kf[pl.ds(r0, COPY_ROWS), :] = k_ref[pl.ds(r0, COPY_ROWS), :].astype(F32)
                vf[pl.ds(r0, COPY_ROWS), :] = v_ref[pl.ds(r0, COPY_ROWS), :].astype(F32)
                return carry
            lax.fori_loop(0, S // COPY_ROWS, widen, 0)

            if d % (ATTN_SPLIT_STRIDE ** 2) == 0:
                s1 = ATTN_SPLIT_STRIDE
                s2 = d // s1
                part = S // s1
                for src_ref, tmp_ref in ((qf, qt), (kf, kt), (vf, vt)):
                    for r1 in range(s1):
                        tmp_ref[r1 * part:(r1 + 1) * part, :] = src_ref[pl.ds(r1, part, stride=s1), :]
                strided = (qt, kt, vt)
                start = lambda r, c: (r % s1) * part + r // s1 + s2 * c * T
                stride = s2
            else:
                strided = (qf, kf, vf)
                start = lambda r, c: r + d * c * T
                stride = d

            def gather(r, carry, L=L, P=P, pad=pad, strided=strided, start=start, stride=stride):
                for c in range(L // T):
                    src = pl.ds(start(r, c), T, stride=stride)
                    qs[pl.ds(pl.multiple_of(r * L + c * T, T), T), :] = strided[0][src, :].astype(BF16)
                    dst = pl.ds(pl.multiple_of(r * P + (pad + c * T), HB), T)
                    kp[dst, :] = strided[1][src, :].astype(BF16)
                    vp[dst, :] = strided[2][src, :].astype(BF16)
                return carry
            lax.fori_loop(0, d, gather, 0, unroll=2)
            q_src = qs

        def geometry(n):
            r, i = divmod(n, nb)
            win = slice(r * P + i * T, r * P + i * T + Wg)
            dst = slice(n * T, (n + 1) * T) if d == 1 else pl.ds(r + d * i * T, T, stride=d)
            return i, win, dst

        for n in range(n_blocks):
            i, win, _ = geometry(n)
            s = lax.dot_general(q_src[n * T:(n + 1) * T, :], kp[win, :], (((1,), (1,)), ((), ())),
                                preferred_element_type=F32)
            s = s + bias_ref[:, 0:Wg]
            kidx = c_row + (i * T - HB)
            if pad and i == 0:
                s = s + jnp.where(kidx >= 0, 0.0, NEG_INF)
            if pad and i == nb - 1:
                s = s + jnp.where(kidx < L, 0.0, NEG_INF)
            s_buf[n, :, 0:Wg] = s
        for n in range(n_blocks):
            _, _, dst = geometry(n)
            m = jnp.max(s_buf[n, :, 0:Wg], axis=-1, keepdims=True)
            p = jnp.exp(s_buf[n, :, 0:Wg] - m)
            den = jnp.sum(p, axis=-1, keepdims=True)
            p_buf[n, :, 0:Wg] = p.astype(BF16)
            rd_buf[n] = jnp.broadcast_to(1.0 / den, (T, D))
            lg[dst, :] = jnp.broadcast_to(m + jnp.log(den), (T, D))
        for n in range(n_blocks):
            _, win, dst = geometry(n)
            og[dst, :] = jnp.dot(p_buf[n, :, 0:Wg], vp[win, :], preferred_element_type=F32) * rd_buf[n]

    def mix(i, carry):
        r0 = pl.multiple_of(i * COPY_ROWS, COPY_ROWS)
        rows = pl.ds(r0, COPY_ROWS)
        l0, l1, l2 = lg0[rows, :], lg1[rows, :], lg2[rows, :]
        m = jnp.maximum(jnp.maximum(l0, l1), l2)
        w0, w1, w2 = jnp.exp(l0 - m), jnp.exp(l1 - m), jnp.exp(l2 - m)
        y = (w0 * og0[rows, :] + w1 * og1[rows, :] + w2 * og2[rows, :]) / (w0 + w1 + w2)
        o_ref[rows, :] = y.astype(o_ref.dtype)
        return carry
    lax.fori_loop(0, S // COPY_ROWS, mix, 0)


def dilated_mixture_attention(qkv):
    B, S, _ = qkv.shape
    D = HEAD_DIM
    T = ATTN_Q_ROWS
    assert all(S % (d * T) == 0 and w // (2 * d) == BAND_HALF for w, d in DIL_GROUPS)
    assert S % COPY_ROWS == 0
    max_d = max(d for _, d in DIL_GROUPS)

    def head_spec(part, g):
        return pl.BlockSpec((None, S, D),
                            lambda b, j: (b, 0, part * N_ATTN_HEADS + HEADS_PER_GROUP * g + j))
    in_specs = [head_spec(part, g) for g in range(len(DIL_GROUPS)) for part in range(3)]
    seq_f32 = pltpu.VMEM((S, D), F32)
    return pl.pallas_call(
        functools.partial(_attn_kernel, S=S),
        grid=(B, HEADS_PER_GROUP),
        in_specs=in_specs,
        out_specs=pl.BlockSpec((None, S, D), lambda b, j: (b, 0, j)),
        out_shape=jax.ShapeDtypeStruct((B, S, HEADS_PER_GROUP * D), BF16),
        scratch_shapes=[
            seq_f32, seq_f32, seq_f32,
            seq_f32, seq_f32, seq_f32,
            pltpu.VMEM((S, D), BF16),
            pltpu.VMEM((S + 2 * BAND_HALF * max_d + BAND_HALF, D), BF16),
            pltpu.VMEM((S + 2 * BAND_HALF * max_d + BAND_HALF, D), BF16),
            seq_f32, seq_f32, seq_f32,
            seq_f32, seq_f32, seq_f32,
            pltpu.VMEM((T, T + 2 * BAND_HALF), F32),
            pltpu.VMEM((S // T, T, T + 2 * BAND_HALF), F32),
            pltpu.VMEM((S // T, T, T + 2 * BAND_HALF), BF16),
            pltpu.VMEM((S // T, T, D), F32),
        ],
        compiler_params=_params(2),
        name="dilated_attention",
    )(*([qkv] * 9))


def _merge_kernel(c_ref, a_ref, ga_ref, gb_ref, x_ref, wc_hbm, wa_hbm, wo_hbm, g_ref, o_ref,
                  wc_ref, wa_ref, wo_ref, stage_ref, sem, *, layer):
    @pl.when(pl.program_id(0) == 0)
    def _():
        _load_weight_bf16(wc_hbm, layer, wc_ref, stage_ref, sem)
        _load_weight_bf16(wa_hbm, layer, wa_ref, stage_ref, sem)
        _load_weight_bf16(wo_hbm, layer, wo_ref, stage_ref, sem)

    yc = jnp.dot(c_ref[...], wc_ref[...], preferred_element_type=F32)
    ya = jnp.dot(a_ref[...], wa_ref[...], preferred_element_type=F32)
    merged = ga_ref[...].astype(F32) * yc + gb_ref[...].astype(F32) * ya
    y = jnp.dot(merged.astype(BF16), wo_ref[...], preferred_element_type=F32)
    o_ref[...] = x_ref[...] + _rms(y, g_ref[...])


def merge_out(c, a, gates, x, wc, wa, wo, layer, g):
    M, D = x.shape
    Cc, Ca = c.shape[1], a.shape[1]
    bm = ROW_TILE
    assert M % bm == 0
    return pl.pallas_call(
        functools.partial(_merge_kernel, layer=layer),
        grid=(M // bm,),
        in_specs=[
            pl.BlockSpec((bm, Cc), lambda i: (i, 0)),
            pl.BlockSpec((bm, Ca), lambda i: (i, 0)),
            pl.BlockSpec((bm, D), lambda i: (i, 0)),
            pl.BlockSpec((bm, D), lambda i: (i, 1)),
            pl.BlockSpec((bm, D), lambda i: (i, 0)),
            _HBM, _HBM, _HBM, _resident((1, D)),
        ],
        out_specs=pl.BlockSpec((bm, D), lambda i: (i, 0)),
        out_shape=jax.ShapeDtypeStruct((M, D), F32),
        scratch_shapes=[pltpu.VMEM((Cc, D), BF16), pltpu.VMEM((Ca, D), BF16), pltpu.VMEM((D, D), BF16)]
        + _stage_scratch(D),
        compiler_params=_params(1),
        name="merge_out",
    )(c, a, gates, gates, x, wc, wa, wo, g.reshape(1, D))


def _cross_kernel(x_ref, k_ref, v_ref, wq_hbm, wo_hbm, gpre_ref, gpost_ref, gn_ref, o_ref, hn_ref,
                  wq_ref, wo_ref, stage_ref, sem, *, D, layer):
    dh = D // X_HEADS

    @pl.when((pl.program_id(0) == 0) & (pl.program_id(1) == 0))
    def _():
        _load_weight_bf16(wq_hbm, layer, wq_ref, stage_ref, sem)
        _load_weight_bf16(wo_hbm, layer, wo_ref, stage_ref, sem)

    heads = [slice(hd * dh, (hd + 1) * dh) for hd in range(X_HEADS)]
    h = _rms(x_ref[...], gpre_ref[...]).astype(BF16)
    q = (jnp.dot(h, wq_ref[...], preferred_element_type=F32) * (dh ** -0.5)).astype(BF16)
    scores = [lax.dot_general(q[:, cols], k_ref[:, cols], (((1,), (1,)), ((), ())),
                              preferred_element_type=F32) for cols in heads]
    probs = []
    for s in scores:
        p = jnp.exp(s - jnp.max(s, axis=-1, keepdims=True))
        probs.append((p * (1.0 / jnp.sum(p, axis=-1, keepdims=True))).astype(BF16))
    ctx = [jnp.dot(p, v_ref[:, cols], preferred_element_type=F32).astype(BF16)
           for p, cols in zip(probs, heads)]
    y = jnp.dot(jnp.concatenate(ctx, axis=-1), wo_ref[...], preferred_element_type=F32)
    xn = x_ref[...] + _rms(y, gpost_ref[...])
    o_ref[...] = xn
    hn_ref[...] = _rms(xn, gn_ref[...]).astype(BF16)


def cross_attention_sublayer(x, kmem, vmem, wq, wo, layer, g_pre, g_post, g_next):
    B, S, D = x.shape
    Nm = kmem.shape[1]
    bm = ROW_TILE
    assert S % bm == 0
    tile = pl.BlockSpec((None, bm, D), lambda b, i: (b, i, 0))
    mem_spec = pl.BlockSpec((None, Nm, D), lambda b, i: (b, 0, 0))
    vec = lambda g: g.reshape(1, D)
    return pl.pallas_call(
        functools.partial(_cross_kernel, D=D, layer=layer),
        grid=(B, S // bm),
        in_specs=[tile, mem_spec, mem_spec, _HBM, _HBM,
                  _resident((1, D)), _resident((1, D)), _resident((1, D))],
        out_specs=(tile, tile),
        out_shape=(jax.ShapeDtypeStruct((B, S, D), F32), jax.ShapeDtypeStruct((B, S, D), BF16)),
        scratch_shapes=[pltpu.VMEM((D, D), BF16), pltpu.VMEM((D, D), BF16)] + _stage_scratch(D),
        compiler_params=_params(2),
        name="cross_attention",
    )(x, kmem, vmem, wq, wo, vec(g_pre), vec(g_post), vec(g_next))


def kernel(x, mem, w_in, w_dw, b_dw, conv_ln_g, conv_ln_b, w_conv_out, w_attn_proj, w_o, g_mix_pre, g_mix_post, g_mem, w_cq, w_ck, w_cv, w_co, g_x_pre, g_x_post, w_ffn_gate, w_ffn_up, w_ffn_down, g_ffn_pre, g_ffn_post):
    B, S, D = x.shape
    depth = w_in.shape[0]
    Cc = w_dw.shape[-1]
    Wq = N_ATTN_HEADS * HEAD_DIM
    Nm = mem.shape[1]
    d_ff = w_ffn_gate.shape[-1]
    c0 = 2 * Cc
    c3 = c0 + 3 * Wq
    M = B * S

    q_scale = jnp.concatenate([jnp.full((Wq,), HEAD_DIM ** -0.5, F32), jnp.ones((2 * Wq,), F32)])
    mem2 = mem.reshape(B * Nm, D)
    x2 = x.reshape(M, D)
    h = rms_norm_bf16(x2, g_mix_pre[0])
    for l in range(depth):
        glu = project_gated(h, (w_in, l, 0), (w_in, l, Cc), Cc, mode="glu")
        qkv = project(h, w_in, l, c0, 3 * Wq, PROJ_TILE_QKV, act="colscale", colscale=q_scale)
        gates = project(h, w_in, l, c3, 2 * D, PROJ_TILE_WIDE, act="sigmoid")
        conv = conv_branch(glu.reshape(B, S, Cc), w_dw[l], b_dw[l], conv_ln_g[l], conv_ln_b[l])
        attn = dilated_mixture_attention(qkv.reshape(B, S, 3 * Wq))
        x2 = merge_out(conv.reshape(M, Cc), attn.reshape(M, -1), gates, x2,
                       w_conv_out, w_attn_proj, w_o, l, g_mix_post[l])
        kmem, vmem = memory_kv(mem2, g_mem[l], w_ck, w_cv, l)
        x3, h3 = cross_attention_sublayer(x2.reshape(B, S, D), kmem.reshape(B, Nm, D), vmem.reshape(B, Nm, D),
                                          w_cq, w_co, l, g_x_pre[l], g_x_post[l], g_ffn_pre[l])
        x2, h = x3.reshape(M, D), h3.reshape(M, D)
        act = project_gated(h, (w_ffn_gate, l, 0), (w_ffn_up, l, 0), d_ff, mode="swiglu")
        x2, h = down_residual(act, w_ffn_down, l, x2, g_ffn_post[l],
                              g_mix_pre[l + 1] if l + 1 < depth else None)
    return x2.reshape(B, S, D)
```

```python
import functools

import jax
import jax.numpy as jnp
from jax import lax
from jax.experimental import pallas as pl
from jax.experimental.pallas import tpu as pltpu

F32 = jnp.float32
BF16 = jnp.bfloat16

EPS = 1e-6
NEG_INF = -1e30

HEAD_DIM = 128
DIL_GROUPS = ((128, 1), (512, 4), (2048, 16))
HEADS_PER_GROUP = 4
N_ATTN_HEADS = HEADS_PER_GROUP * len(DIL_GROUPS)
BAND_HALF = 64
X_HEADS = 4

V7X_VMEM_LIMIT_BYTES = 56 * 1024 * 1024
LANES = 128

PROJ_TILE_GATED = (2048, 512)
PROJ_GATED_DOT_ROWS = 1024
PROJ_TILE_QKV = (1024, 1536)
PROJ_TILE_WIDE = (1024, 1024)
ROW_TILE = 256
NORM_ROWS = 1024
NORM_CHUNK = 64
COPY_ROWS = 256


def _params(n_grid_dims):
    return pltpu.CompilerParams(
        dimension_semantics=("arbitrary",) * n_grid_dims,
        vmem_limit_bytes=V7X_VMEM_LIMIT_BYTES)


def _resident(shape):
    return pl.BlockSpec(shape, lambda *_: (0,) * len(shape), pipeline_mode=pl.Buffered(1))


def _rms(y, g):
    ms = jnp.mean(y * y, axis=-1, keepdims=True)
    return y * lax.rsqrt(ms + EPS) * g


WEIGHT_CHUNK_ROWS = 256
_HBM = pl.BlockSpec(memory_space=pl.ANY)


def _stage_scratch(n_cols):
    return [pltpu.VMEM((2, WEIGHT_CHUNK_ROWS, n_cols), F32), pltpu.SemaphoreType.DMA((2,))]


def _load_weight_bf16(w_hbm, layer, wb_ref, stage_ref, sem):
    R = WEIGHT_CHUNK_ROWS
    n_chunks = wb_ref.shape[0] // R
    assert wb_ref.shape[0] % R == 0

    def copy(c):
        return pltpu.make_async_copy(w_hbm.at[layer, pl.ds(c * R, R), :], stage_ref.at[c % 2], sem.at[c % 2])

    copy(0).start()
    for c in range(n_chunks):
        if c + 1 < n_chunks:
            copy(c + 1).start()
        copy(c).wait()
        wb_ref[c * R:(c + 1) * R, :] = stage_ref[c % 2].astype(BF16)


def _rms_rows_to_bf16(x_ref, g_ref, o_ref, rows):
    def body(i, carry):
        r = pl.multiple_of(i * NORM_CHUNK, NORM_CHUNK)
        o_ref[pl.ds(r, NORM_CHUNK), :] = _rms(x_ref[pl.ds(r, NORM_CHUNK), :], g_ref[...]).astype(BF16)
        return carry
    lax.fori_loop(0, rows // NORM_CHUNK, body, 0)


def _norm_kernel(x_ref, g_ref, o_ref, *, bm):
    _rms_rows_to_bf16(x_ref, g_ref, o_ref, bm)


def rms_norm_bf16(x, g):
    M, K = x.shape
    bm = NORM_ROWS
    assert M % bm == 0
    return pl.pallas_call(
        functools.partial(_norm_kernel, bm=bm),
        grid=(M // bm,),
        in_specs=[pl.BlockSpec((bm, K), lambda i: (i, 0)), pl.BlockSpec((1, K), lambda i: (0, 0))],
        out_specs=pl.BlockSpec((bm, K), lambda i: (i, 0)),
        out_shape=jax.ShapeDtypeStruct((M, K), BF16),
        compiler_params=_params(1),
        name="rms_norm",
    )(x, g.reshape(1, K))


def _weight_spec(w_stack, layer, col0, bn):
    K = w_stack.shape[1]
    assert col0 % LANES == 0 and bn % LANES == 0
    return pl.BlockSpec((pl.Element(1), pl.Element(K), pl.Element(bn)),
                        lambda j, i: (layer, 0, pl.multiple_of(col0 + j * bn, LANES)))


def _proj_kernel(*refs, act):
    if act == "colscale":
        h_ref, w_ref, cs_ref, o_ref, wb_ref = refs
    else:
        h_ref, w_ref, o_ref, wb_ref = refs

    @pl.when(pl.program_id(1) == 0)
    def _():
        wb_ref[...] = w_ref[0].astype(BF16)

    acc = jnp.dot(h_ref[...], wb_ref[...], preferred_element_type=F32)
    if act == "sigmoid":
        acc = jax.nn.sigmoid(acc)
    elif act == "colscale":
        acc = acc * cs_ref[...]
    o_ref[...] = acc.astype(o_ref.dtype)


def project(h, w_stack, layer, col0, n_cols, tile, *, act, colscale=None):
    M, K = h.shape
    bm, bn = tile
    assert M % bm == 0 and n_cols % bn == 0 and (act == "colscale") == (colscale is not None)
    in_specs = [pl.BlockSpec((bm, K), lambda j, i: (i, 0)), _weight_spec(w_stack, layer, col0, bn)]
    args = [h, w_stack]
    if colscale is not None:
        in_specs.append(pl.BlockSpec((1, bn), lambda j, i: (0, j)))
        args.append(colscale.reshape(1, n_cols))
    return pl.pallas_call(
        functools.partial(_proj_kernel, act=act),
        grid=(n_cols // bn, M // bm),
        in_specs=in_specs,
        out_specs=pl.BlockSpec((bm, bn), lambda j, i: (i, j)),
        out_shape=jax.ShapeDtypeStruct((M, n_cols), BF16),
        scratch_shapes=[pltpu.VMEM((K, bn), BF16)],
        compiler_params=_params(2),
        name="project_" + act,
    )(*args)


def _proj_gated_kernel(h_ref, wa_ref, wb_ref, o_ref, wa_bf, wb_bf, *, mode):
    @pl.when(pl.program_id(1) == 0)
    def _():
        wa_bf[...] = wa_ref[0].astype(BF16)
        wb_bf[...] = wb_ref[0].astype(BF16)

    for r0 in range(0, h_ref.shape[0], PROJ_GATED_DOT_ROWS):
        rows = slice(r0, r0 + PROJ_GATED_DOT_ROWS)
        h = h_ref[rows, :]
        a = jnp.dot(h, wa_bf[...], preferred_element_type=F32)
        b = jnp.dot(h, wb_bf[...], preferred_element_type=F32)
        if mode == "glu":
            y = a * jax.nn.sigmoid(b)
        else:
            y = a * jax.nn.sigmoid(a) * b
        o_ref[rows, :] = y.astype(o_ref.dtype)


def project_gated(h, wa, wb, n_cols, *, mode):
    M, K = h.shape
    bm, bn = PROJ_TILE_GATED
    assert M % bm == 0 and n_cols % bn == 0
    return pl.pallas_call(
        functools.partial(_proj_gated_kernel, mode=mode),
        grid=(n_cols // bn, M // bm),
        in_specs=[pl.BlockSpec((bm, K), lambda j, i: (i, 0)), _weight_spec(*wa, bn), _weight_spec(*wb, bn)],
        out_specs=pl.BlockSpec((bm, bn), lambda j, i: (i, j)),
        out_shape=jax.ShapeDtypeStruct((M, n_cols), BF16),
        scratch_shapes=[pltpu.VMEM((K, bn), BF16), pltpu.VMEM((K, bn), BF16)],
        compiler_params=_params(2),
        name="project_" + mode,
    )(h, wa[0], wb[0])


MEM_KV_COLS = 256


def _mem_kv_kernel(mem_ref, g_ref, wk_ref, wv_ref, k_ref, v_ref, memn_ref, *, rows):
    @pl.when(pl.program_id(0) == 0)
    def _():
        _rms_rows_to_bf16(mem_ref, g_ref, memn_ref, rows)

    memn = memn_ref[...]
    k_ref[...] = jnp.dot(memn, wk_ref[0].astype(BF16), preferred_element_type=F32).astype(BF16)
    v_ref[...] = jnp.dot(memn, wv_ref[0].astype(BF16), preferred_element_type=F32).astype(BF16)


def memory_kv(mem, g, wk_stack, wv_stack, layer):
    R, K = mem.shape
    N = wk_stack.shape[2]
    bn = MEM_KV_COLS
    assert N % bn == 0 and R % NORM_CHUNK == 0
    w_spec = pl.BlockSpec((pl.Element(1), pl.Element(K), pl.Element(bn)),
                          lambda j: (layer, 0, pl.multiple_of(j * bn, LANES)))
    out_spec = pl.BlockSpec((R, bn), lambda j: (0, j))
    out = jax.ShapeDtypeStruct((R, N), BF16)
    return pl.pallas_call(
        functools.partial(_mem_kv_kernel, rows=R),
        grid=(N // bn,),
        in_specs=[_resident((R, K)), _resident((1, K)), w_spec, w_spec],
        out_specs=(out_spec, out_spec),
        out_shape=(out, out),
        scratch_shapes=[pltpu.VMEM((R, K), BF16)],
        compiler_params=_params(1),
        name="memory_kv",
    )(mem, g.reshape(1, K), wk_stack, wv_stack)


def _down_kernel(*refs, layer, emit_next):
    if emit_next:
        a_ref, w_hbm, x_ref, g_ref, gn_ref, o_ref, hn_ref, w_ref, stage_ref, sem = refs
    else:
        a_ref, w_hbm, x_ref, g_ref, o_ref, w_ref, stage_ref, sem = refs

    @pl.when(pl.program_id(0) == 0)
    def _():
        _load_weight_bf16(w_hbm, layer, w_ref, stage_ref, sem)

    y = jnp.dot(a_ref[...], w_ref[...], preferred_element_type=F32)
    xn = x_ref[...] + _rms(y, g_ref[...])
    o_ref[...] = xn
    if emit_next:
        hn_ref[...] = _rms(xn, gn_ref[...]).astype(BF16)


def down_residual(a, w_stack, layer, x, g, g_next):
    M, K = a.shape
    N = w_stack.shape[2]
    bm = ROW_TILE
    emit_next = g_next is not None
    assert M % bm == 0
    tile = pl.BlockSpec((bm, N), lambda i: (i, 0))
    in_specs = [pl.BlockSpec((bm, K), lambda i: (i, 0)), _HBM, tile, _resident((1, N))]
    args = [a, w_stack, x, g.reshape(1, N)]
    out_specs, out_shape = tile, jax.ShapeDtypeStruct((M, N), F32)
    if emit_next:
        in_specs.append(_resident((1, N)))
        args.append(g_next.reshape(1, N))
        out_specs, out_shape = (tile, tile), (out_shape, jax.ShapeDtypeStruct((M, N), BF16))
    res = pl.pallas_call(
        functools.partial(_down_kernel, layer=layer, emit_next=emit_next),
        grid=(M // bm,),
        in_specs=in_specs,
        out_specs=out_specs,
        out_shape=out_shape,
        scratch_shapes=[pltpu.VMEM((K, N), BF16)] + _stage_scratch(N),
        compiler_params=_params(1),
        name="down_residual",
    )(*args)
    return res if emit_next else (res, None)


CONV_ROWS = 128
CONV_LN_ROWS = 8
CONV_LN_INTERLEAVE = 16
CONV_DATA_OFF = 16


def _conv_kernel(h_ref, wdw_ref, bdw_ref, lng_ref, lnb_ref, o_ref, hp_ref, cv_ref, *, S, C, KW):
    pad = KW // 2
    off = CONV_DATA_OFF
    n_slab = C // LANES
    R = CONV_ROWS

    for j in range(n_slab):
        cols = slice(j * LANES, (j + 1) * LANES)
        hp_ref[j, 0:off, :] = jnp.zeros((off, LANES), F32)
        hp_ref[j, off + S:off + S + off, :] = jnp.zeros((off, LANES), F32)

        def copy(i, carry, j=j, cols=cols):
            r = pl.multiple_of(i * COPY_ROWS, COPY_ROWS)
            hp_ref[j, pl.ds(off + r, COPY_ROWS), :] = h_ref[pl.ds(r, COPY_ROWS), cols].astype(F32)
            return carry
        lax.fori_loop(0, S // COPY_ROWS, copy, 0)

        def block(i, carry, j=j, cols=cols):
            for phase in range(2):
                t0 = i * (2 * R) + phase
                acc = jnp.broadcast_to(bdw_ref[:, cols], (R, LANES))
                for k in range(KW):
                    taps = hp_ref[j, pl.ds(t0 + (off - pad + k), R, stride=2), :]
                    acc = acc + taps * wdw_ref[k:k + 1, cols]
                cv_ref[j, pl.ds(t0, R, stride=2), :] = acc
            return carry
        lax.fori_loop(0, S // (2 * R), block, 0)

    def norm(i, carry):
        for u in range(CONV_LN_INTERLEAVE):
            start = (i * CONV_LN_INTERLEAVE + u) * CONV_LN_ROWS
            rows = pl.ds(pl.multiple_of(start, CONV_LN_ROWS), CONV_LN_ROWS)
            xs = [cv_ref[j, rows, :] for j in range(n_slab)]
            mu = jnp.sum(sum(xs), axis=-1, keepdims=True) * (1.0 / C)
            xs = [x - mu for x in xs]
            var = jnp.sum(sum(x * x for x in xs), axis=-1, keepdims=True) * (1.0 / C)
            inv = lax.rsqrt(var + EPS)
            for j in range(n_slab):
                cols = slice(j * LANES, (j + 1) * LANES)
                y = xs[j] * inv * lng_ref[:, cols] + lnb_ref[:, cols]
                o_ref[rows, cols] = (y * jax.nn.sigmoid(y)).astype(o_ref.dtype)
        return carry
    lax.fori_loop(0, S // (CONV_LN_ROWS * CONV_LN_INTERLEAVE), norm, 0)


def conv_branch(h, w_dw, b_dw, ln_g, ln_b):
    B, S, C = h.shape
    KW = w_dw.shape[0]
    assert KW // 2 < CONV_DATA_OFF and S % (2 * CONV_ROWS) == 0 and C % LANES == 0
    row = lambda v: v.reshape(1, C)
    vec = pl.BlockSpec((1, C), lambda b: (0, 0))
    return pl.pallas_call(
        functools.partial(_conv_kernel, S=S, C=C, KW=KW),
        grid=(B,),
        in_specs=[
            pl.BlockSpec((None, S, C), lambda b: (b, 0, 0)),
            pl.BlockSpec((KW, C), lambda b: (0, 0)),
            vec, vec, vec,
        ],
        out_specs=pl.BlockSpec((None, S, C), lambda b: (b, 0, 0)),
        out_shape=jax.ShapeDtypeStruct((B, S, C), BF16),
        scratch_shapes=[pltpu.VMEM((C // LANES, S + 2 * CONV_DATA_OFF, LANES), F32),
                        pltpu.VMEM((C // LANES, S, LANES), F32)],
        compiler_params=_params(1),
        name="conv_branch",
    )(h, w_dw, row(b_dw), row(ln_g), row(ln_b))


ATTN_Q_ROWS = 128
ATTN_SPLIT_STRIDE = 4


def _attn_kernel(q0, k0, v0, q1, k1, v1, q2, k2, v2, o_ref,
                 qf, kf, vf, qt, kt, vt, qs, kp, vp, og0, og1, og2, lg0, lg1, lg2, bias_ref, s_buf, p_buf, rd_buf, *, S):
    T = ATTN_Q_ROWS
    HB = BAND_HALF
    W = T + 2 * HB
    D = HEAD_DIM
    n_blocks = S // T
    qkv = ((q0, k0, v0), (q1, k1, v1), (q2, k2, v2))
    outs = ((og0, lg0), (og1, lg1), (og2, lg2))
    slot_f = jnp.full((T, W), pl.program_id(1), jnp.int32).astype(F32)

    a_idx = lax.broadcasted_iota(jnp.int32, (T, W), 0)
    c_idx = lax.broadcasted_iota(jnp.int32, (T, W), 1)
    c_row = lax.broadcasted_iota(jnp.int32, (1, W), 1)

    for g, (_, d) in enumerate(DIL_GROUPS):
        L = S // d
        nb = L // T
        pad = HB if nb > 1 else 0
        P = L + 2 * pad
        Wg = T + 2 * pad
        q_ref, k_ref, v_ref = qkv[g]
        og, lg = outs[g]

        slope = jnp.exp2(-8.0 * (slot_f + (HEADS_PER_GROUP * g + 1.0)) / N_ATTN_HEADS)
        rel = jnp.abs(c_idx - pad - a_idx)
        bias = jnp.where(rel <= HB, -(slope * (d * rel).astype(F32)), NEG_INF)
        bias_ref[:, 0:Wg] = bias[:, 0:Wg]

        if pad:
            kp[0:HB, :] = jnp.zeros((HB, D), BF16)
            vp[0:HB, :] = jnp.zeros((HB, D), BF16)

            def clear(r, carry, L=L, P=P):
                r0 = pl.multiple_of(r * P + (HB + L), HB)
                kp[pl.ds(r0, 2 * HB), :] = jnp.zeros((2 * HB, D), BF16)
                vp[pl.ds(r0, 2 * HB), :] = jnp.zeros((2 * HB, D), BF16)
                return carry
            lax.fori_loop(0, d, clear, 0)

        if d == 1:
            def copy(i, carry, k_ref=k_ref, v_ref=v_ref):
                r0 = pl.multiple_of(i * COPY_ROWS, COPY_ROWS)
                kp[pl.ds(HB + r0, COPY_ROWS), :] = k_ref[pl.ds(r0, COPY_ROWS), :]
                vp[pl.ds(HB + r0, COPY_ROWS), :] = v_ref[pl.ds(r0, COPY_ROWS), :]
                return carry
            lax.fori_loop(0, S // COPY_ROWS, copy, 0)
            q_src = q_ref
        else:
            def widen(i, carry, q_ref=q_ref, k_ref=k_ref, v_ref=v_ref):
                r0 = pl.multiple_of(i * COPY_ROWS, COPY_ROWS)
                qf[pl.ds(r0, COPY_ROWS), :] = q_ref[pl.ds(r0, COPY_ROWS), :].astype(F32)
                kf[pl.ds(r0, COPY_ROWS), :] = k_ref[pl.ds(r0, COPY_ROWS), :].astype(F32)
                vf[pl.ds(r0, COPY_ROWS), :] = v_ref[pl.ds(r0, COPY_ROWS), :].astype(F32)
                return carry
            lax.fori_loop(0, S // COPY_ROWS, widen, 0)

            if d % (ATTN_SPLIT_STRIDE ** 2) == 0:
                s1 = ATTN_SPLIT_STRIDE
                s2 = d // s1
                part = S // s1
                for src_ref, tmp_ref in ((qf, qt), (kf, kt), (vf, vt)):
                    for r1 in range(s1):
                        tmp_ref[r1 * part:(r1 + 1) * part, :] = src_ref[pl.ds(r1, part, stride=s1), :]
                strided = (qt, kt, vt)
                start = lambda r, c: (r % s1) * part + r // s1 + s2 * c * T
                stride = s2
            else:
                strided = (qf, kf, vf)
                start = lambda r, c: r + d * c * T
                stride = d

            def gather(r, carry, L=L, P=P, pad=pad, strided=strided, start=start, stride=stride):
                for c in range(L // T):
                    src = pl.ds(start(r, c), T, stride=stride)
                    qs[pl.ds(pl.multiple_of(r * L + c * T, T), T), :] = strided[0][src, :].astype(BF16)
                    dst = pl.ds(pl.multiple_of(r * P + (pad + c * T), HB), T)
                    kp[dst, :] = strided[1][src, :].astype(BF16)
                    vp[dst, :] = strided[2][src, :].astype(BF16)
                return carry
            lax.fori_loop(0, d, gather, 0, unroll=2)
            q_src = qs

        def geometry(n):
            r, i = divmod(n, nb)
            win = slice(r * P + i * T, r * P + i * T + Wg)
            dst = slice(n * T, (n + 1) * T) if d == 1 else pl.ds(r + d * i * T, T, stride=d)
            return i, win, dst

        for n in range(n_blocks):
            i, win, _ = geometry(n)
            s = lax.dot_general(q_src[n * T:(n + 1) * T, :], kp[win, :], (((1,), (1,)), ((), ())),
                                preferred_element_type=F32)
            s = s + bias_ref[:, 0:Wg]
            kidx = c_row + (i * T - HB)
            if pad and i == 0:
                s = s + jnp.where(kidx >= 0, 0.0, NEG_INF)
            if pad and i == nb - 1:
                s = s + jnp.where(kidx < L, 0.0, NEG_INF)
            s_buf[n, :, 0:Wg] = s
        for n in range(n_blocks):
            _, _, dst = geometry(n)
            m = jnp.max(s_buf[n, :, 0:Wg], axis=-1, keepdims=True)
            p = jnp.exp(s_buf[n, :, 0:Wg] - m)
            den = jnp.sum(p, axis=-1, keepdims=True)
            p_buf[n, :, 0:Wg] = p.astype(BF16)
            rd_buf[n] = jnp.broadcast_to(1.0 / den, (T, D))
            lg[dst, :] = jnp.broadcast_to(m + jnp.log(den), (T, D))
        for n in range(n_blocks):
            _, win, dst = geometry(n)
            og[dst, :] = jnp.dot(p_buf[n, :, 0:Wg], vp[win, :], preferred_element_type=F32) * rd_buf[n]

    def mix(i, carry):
        r0 = pl.multiple_of(i * COPY_ROWS, COPY_ROWS)
        rows = pl.ds(r0, COPY_ROWS)
        l0, l1, l2 = lg0[rows, :], lg1[rows, :], lg2[rows, :]
        m = jnp.maximum(jnp.maximum(l0, l1), l2)
        w0, w1, w2 = jnp.exp(l0 - m), jnp.exp(l1 - m), jnp.exp(l2 - m)
        y = (w0 * og0[rows, :] + w1 * og1[rows, :] + w2 * og2[rows, :]) / (w0 + w1 + w2)
        o_ref[rows, :] = y.astype(o_ref.dtype)
        return carry
    lax.fori_loop(0, S // COPY_ROWS, mix, 0)


def dilated_mixture_attention(qkv):
    B, S, _ = qkv.shape
    D = HEAD_DIM
    T = ATTN_Q_ROWS
    assert all(S % (d * T) == 0 and w // (2 * d) == BAND_HALF for w, d in DIL_GROUPS)
    assert S % COPY_ROWS == 0
    max_d = max(d for _, d in DIL_GROUPS)

    def head_spec(part, g):
        return pl.BlockSpec((None, S, D),
                            lambda b, j: (b, 0, part * N_ATTN_HEADS + HEADS_PER_GROUP * g + j))
    in_specs = [head_spec(part, g) for g in range(len(DIL_GROUPS)) for part in range(3)]
    seq_f32 = pltpu.VMEM((S, D), F32)
    return pl.pallas_call(
        functools.partial(_attn_kernel, S=S),
        grid=(B, HEADS_PER_GROUP),
        in_specs=in_specs,
        out_specs=pl.BlockSpec((None, S, D), lambda b, j: (b, 0, j)),
        out_shape=jax.ShapeDtypeStruct((B, S, HEADS_PER_GROUP * D), BF16),
        scratch_shapes=[
            seq_f32, seq_f32, seq_f32,
            seq_f32, seq_f32, seq_f32,
            pltpu.VMEM((S, D), BF16),
            pltpu.VMEM((S + 2 * BAND_HALF * max_d + BAND_HALF, D), BF16),
            pltpu.VMEM((S + 2 * BAND_HALF * max_d + BAND_HALF, D), BF16),
            seq_f32, seq_f32, seq_f32,
            seq_f32, seq_f32, seq_f32,
            pltpu.VMEM((T, T + 2 * BAND_HALF), F32),
            pltpu.VMEM((S // T, T, T + 2 * BAND_HALF), F32),
            pltpu.VMEM((S // T, T, T + 2 * BAND_HALF), BF16),
            pltpu.VMEM((S // T, T, D), F32),
        ],
        compiler_params=_params(2),
        name="dilated_attention",
    )(*([qkv] * 9))


def _merge_kernel(c_ref, a_ref, ga_ref, gb_ref, x_ref, wc_hbm, wa_hbm, wo_hbm, g_ref, o_ref,
                  wc_ref, wa_ref, wo_ref, stage_ref, sem, *, layer):
    @pl.when(pl.program_id(0) == 0)
    def _():
        _load_weight_bf16(wc_hbm, layer, wc_ref, stage_ref, sem)
        _load_weight_bf16(wa_hbm, layer, wa_ref, stage_ref, sem)
        _load_weight_bf16(wo_hbm, layer, wo_ref, stage_ref, sem)

    yc = jnp.dot(c_ref[...], wc_ref[...], preferred_element_type=F32)
    ya = jnp.dot(a_ref[...], wa_ref[...], preferred_element_type=F32)
    merged = ga_ref[...].astype(F32) * yc + gb_ref[...].astype(F32) * ya
    y = jnp.dot(merged.astype(BF16), wo_ref[...], preferred_element_type=F32)
    o_ref[...] = x_ref[...] + _rms(y, g_ref[...])


def merge_out(c, a, gates, x, wc, wa, wo, layer, g):
    M, D = x.shape
    Cc, Ca = c.shape[1], a.shape[1]
    bm = ROW_TILE
    assert M % bm == 0
    return pl.pallas_call(
        functools.partial(_merge_kernel, layer=layer),
        grid=(M // bm,),
        in_specs=[
            pl.BlockSpec((bm, Cc), lambda i: (i, 0)),
            pl.BlockSpec((bm, Ca), lambda i: (i, 0)),
            pl.BlockSpec((bm, D), lambda i: (i, 0)),
            pl.BlockSpec((bm, D), lambda i: (i, 1)),
            pl.BlockSpec((bm, D), lambda i: (i, 0)),
            _HBM, _HBM, _HBM, _resident((1, D)),
        ],
        out_specs=pl.BlockSpec((bm, D), lambda i: (i, 0)),
        out_shape=jax.ShapeDtypeStruct((M, D), F32),
        scratch_shapes=[pltpu.VMEM((Cc, D), BF16), pltpu.VMEM((Ca, D), BF16), pltpu.VMEM((D, D), BF16)]
        + _stage_scratch(D),
        compiler_params=_params(1),
        name="merge_out",
    )(c, a, gates, gates, x, wc, wa, wo, g.reshape(1, D))


def _cross_kernel(x_ref, k_ref, v_ref, wq_hbm, wo_hbm, gpre_ref, gpost_ref, gn_ref, o_ref, hn_ref,
                  wq_ref, wo_ref, stage_ref, sem, *, D, layer):
    dh = D // X_HEADS

    @pl.when((pl.program_id(0) == 0) & (pl.program_id(1) == 0))
    def _():
        _load_weight_bf16(wq_hbm, layer, wq_ref, stage_ref, sem)
        _load_weight_bf16(wo_hbm, layer, wo_ref, stage_ref, sem)

    heads = [slice(hd * dh, (hd + 1) * dh) for hd in range(X_HEADS)]
    h = _rms(x_ref[...], gpre_ref[...]).astype(BF16)
    q = (jnp.dot(h, wq_ref[...], preferred_element_type=F32) * (dh ** -0.5)).astype(BF16)
    scores = [lax.dot_general(q[:, cols], k_ref[:, cols], (((1,), (1,)), ((), ())),
                              preferred_element_type=F32) for cols in heads]
    probs = []
    for s in scores:
        p = jnp.exp(s - jnp.max(s, axis=-1, keepdims=True))
        probs.append((p * (1.0 / jnp.sum(p, axis=-1, keepdims=True))).astype(BF16))
    ctx = [jnp.dot(p, v_ref[:, cols], preferred_element_type=F32).astype(BF16)
           for p, cols in zip(probs, heads)]
    y = jnp.dot(jnp.concatenate(ctx, axis=-1), wo_ref[...], preferred_element_type=F32)
    xn = x_ref[...] + _rms(y, gpost_ref[...])
    o_ref[...] = xn
    hn_ref[...] = _rms(xn, gn_ref[...]).astype(BF16)


def cross_attention_sublayer(x, kmem, vmem, wq, wo, layer, g_pre, g_post, g_next):
    B, S, D = x.shape
    Nm = kmem.shape[1]
    bm = ROW_TILE
    assert S % bm == 0
    tile = pl.BlockSpec((None, bm, D), lambda b, i: (b, i, 0))
    mem_spec = pl.BlockSpec((None, Nm, D), lambda b, i: (b, 0, 0))
    vec = lambda g: g.reshape(1, D)
    return pl.pallas_call(
        functools.partial(_cross_kernel, D=D, layer=layer),
        grid=(B, S // bm),
        in_specs=[tile, mem_spec, mem_spec, _HBM, _HBM,
                  _resident((1, D)), _resident((1, D)), _resident((1, D))],
        out_specs=(tile, tile),
        out_shape=(jax.ShapeDtypeStruct((B, S, D), F32), jax.ShapeDtypeStruct((B, S, D), BF16)),
        scratch_shapes=[pltpu.VMEM((D, D), BF16), pltpu.VMEM((D, D), BF16)] + _stage_scratch(D),
        compiler_params=_params(2),
        name="cross_attention",
    )(x, kmem, vmem, wq, wo, vec(g_pre), vec(g_post), vec(g_next))


def kernel(x, mem, w_in, w_dw, b_dw, conv_ln_g, conv_ln_b, w_conv_out, w_attn_proj, w_o, g_mix_pre, g_mix_post, g_mem, w_cq, w_ck, w_cv, w_co, g_x_pre, g_x_post, w_ffn_gate, w_ffn_up, w_ffn_down, g_ffn_pre, g_ffn_post):
    B, S, D = x.shape
    depth = w_in.shape[0]
    Cc = w_dw.shape[-1]
    Wq = N_ATTN_HEADS * HEAD_DIM
    Nm = mem.shape[1]
    d_ff = w_ffn_gate.shape[-1]
    c0 = 2 * Cc
    c3 = c0 + 3 * Wq
    M = B * S

    q_scale = jnp.concatenate([jnp.full((Wq,), HEAD_DIM ** -0.5, F32), jnp.ones((2 * Wq,), F32)])
    mem2 = mem.reshape(B * Nm, D)
    x2 = x.reshape(M, D)
    h = rms_norm_bf16(x2, g_mix_pre[0])
    for l in range(depth):
        glu = project_gated(h, (w_in, l, 0), (w_in, l, Cc), Cc, mode="glu")
        qkv = project(h, w_in, l, c0, 3 * Wq, PROJ_TILE_QKV, act="colscale", colscale=q_scale)
        gates = project(h, w_in, l, c3, 2 * D, PROJ_TILE_WIDE, act="sigmoid")
        conv = conv_branch(glu.reshape(B, S, Cc), w_dw[l], b_dw[l], conv_ln_g[l], conv_ln_b[l])
        attn = dilated_mixture_attention(qkv.reshape(B, S, 3 * Wq))
        x2 = merge_out(conv.reshape(M, Cc), attn.reshape(M, -1), gates, x2,
                       w_conv_out, w_attn_proj, w_o, l, g_mix_post[l])
        kmem, vmem = memory_kv(mem2, g_mem[l], w_ck, w_cv, l)
        x3, h3 = cross_attention_sublayer(x2.reshape(B, S, D), kmem.reshape(B, Nm, D), vmem.reshape(B, Nm, D),
                                          w_cq, w_co, l, g_x_pre[l], g_x_post[l], g_ffn_pre[l])
        x2, h = x3.reshape(M, D), h3.reshape(M, D)
        act = project_gated(h, (w_ffn_gate, l, 0), (w_ffn_up, l, 0), d_ff, mode="swiglu")
        x2, h = down_residual(act, w_ffn_down, l, x2, g_ffn_post[l],
                              g_mix_pre[l + 1] if l + 1 < depth else None)
    return x2.reshape(B, S, D)
```

```python
import functools

import jax
import jax.numpy as jnp
from jax import lax
from jax.experimental import pallas as pl
from jax.experimental.pallas import tpu as pltpu

F32 = jnp.float32
BF16 = jnp.bfloat16

EPS = 1e-6
NEG_INF = -1e30

HEAD_DIM = 128
DIL_GROUPS = ((128, 1), (512, 4), (2048, 16))
HEADS_PER_GROUP = 4
N_ATTN_HEADS = HEADS_PER_GROUP * len(DIL_GROUPS)
BAND_HALF = 64
X_HEADS = 4

V7X_VMEM_LIMIT_BYTES = 56 * 1024 * 1024
LANES = 128

PROJ_TILE_GATED = (2048, 512)
PROJ_TILE_QKV = (1024, 1536)
PROJ_TILE_WIDE = (2048, 1024)
PROJ_DOT_ROWS = 1024
ROW_TILE = 256
ROW_TILES_PER_STEP = 2
NORM_ROWS = 1024
NORM_CHUNK = 64
COPY_ROWS = 256


def _params(n_grid_dims):
    return pltpu.CompilerParams(
        dimension_semantics=("arbitrary",) * n_grid_dims,
        vmem_limit_bytes=V7X_VMEM_LIMIT_BYTES)


def _resident(shape):
    return pl.BlockSpec(shape, lambda *_: (0,) * len(shape), pipeline_mode=pl.Buffered(1))


def _rms(y, g):
    ms = jnp.mean(y * y, axis=-1, keepdims=True)
    return y * lax.rsqrt(ms + EPS) * g


WEIGHT_CHUNK_ROWS = 256
_HBM = pl.BlockSpec(memory_space=pl.ANY)


def _stage_scratch(n_cols):
    return [pltpu.VMEM((2, WEIGHT_CHUNK_ROWS, n_cols), F32), pltpu.SemaphoreType.DMA((2,))]


def _load_weight_bf16(w_hbm, layer, wb_ref, stage_ref, sem):
    R = WEIGHT_CHUNK_ROWS
    n_chunks = wb_ref.shape[0] // R
    assert wb_ref.shape[0] % R == 0

    def copy(c):
        return pltpu.make_async_copy(w_hbm.at[layer, pl.ds(c * R, R), :], stage_ref.at[c % 2], sem.at[c % 2])

    copy(0).start()
    for c in range(n_chunks):
        if c + 1 < n_chunks:
            copy(c + 1).start()
        copy(c).wait()
        wb_ref[c * R:(c + 1) * R, :] = stage_ref[c % 2].astype(BF16)


def _rms_rows_to_bf16(x_ref, g_ref, o_ref, rows):
    def body(i, carry):
        r = pl.multiple_of(i * NORM_CHUNK, NORM_CHUNK)
        o_ref[pl.ds(r, NORM_CHUNK), :] = _rms(x_ref[pl.ds(r, NORM_CHUNK), :], g_ref[...]).astype(BF16)
        return carry
    lax.fori_loop(0, rows // NORM_CHUNK, body, 0)


def _norm_kernel(x_ref, g_ref, o_ref, *, bm):
    _rms_rows_to_bf16(x_ref, g_ref, o_ref, bm)


def rms_norm_bf16(x, g):
    M, K = x.shape
    bm = NORM_ROWS
    assert M % bm == 0
    return pl.pallas_call(
        functools.partial(_norm_kernel, bm=bm),
        grid=(M // bm,),
        in_specs=[pl.BlockSpec((bm, K), lambda i: (i, 0)), pl.BlockSpec((1, K), lambda i: (0, 0))],
        out_specs=pl.BlockSpec((bm, K), lambda i: (i, 0)),
        out_shape=jax.ShapeDtypeStruct((M, K), BF16),
        compiler_params=_params(1),
        name="rms_norm",
    )(x, g.reshape(1, K))


def _weight_spec(w_stack, layer, col0, bn):
    K = w_stack.shape[1]
    assert col0 % LANES == 0 and bn % LANES == 0
    return pl.BlockSpec((pl.Element(1), pl.Element(K), pl.Element(bn)),
                        lambda j, i: (layer, 0, pl.multiple_of(col0 + j * bn, LANES)))


def _proj_kernel(*refs, act):
    if act == "colscale":
        h_ref, w_ref, cs_ref, o_ref, wb_ref = refs
    else:
        h_ref, w_ref, o_ref, wb_ref = refs

    @pl.when(pl.program_id(1) == 0)
    def _():
        wb_ref[...] = w_ref[0].astype(BF16)

    for r0 in range(0, h_ref.shape[0], PROJ_DOT_ROWS):
        rows = slice(r0, r0 + PROJ_DOT_ROWS)
        acc = jnp.dot(h_ref[rows, :], wb_ref[...], preferred_element_type=F32)
        if act == "sigmoid":
            acc = jax.nn.sigmoid(acc)
        elif act == "colscale":
            acc = acc * cs_ref[...]
        o_ref[rows, :] = acc.astype(o_ref.dtype)


def project(h, w_stack, layer, col0, n_cols, tile, *, act, colscale=None):
    M, K = h.shape
    bm, bn = tile
    assert M % bm == 0 and n_cols % bn == 0 and (act == "colscale") == (colscale is not None)
    in_specs = [pl.BlockSpec((bm, K), lambda j, i: (i, 0)), _weight_spec(w_stack, layer, col0, bn)]
    args = [h, w_stack]
    if colscale is not None:
        in_specs.append(pl.BlockSpec((1, bn), lambda j, i: (0, j)))
        args.append(colscale.reshape(1, n_cols))
    return pl.pallas_call(
        functools.partial(_proj_kernel, act=act),
        grid=(n_cols // bn, M // bm),
        in_specs=in_specs,
        out_specs=pl.BlockSpec((bm, bn), lambda j, i: (i, j)),
        out_shape=jax.ShapeDtypeStruct((M, n_cols), BF16),
        scratch_shapes=[pltpu.VMEM((K, bn), BF16)],
        compiler_params=_params(2),
        name="project_" + act,
    )(*args)


def _proj_gated_kernel(h_ref, wa_ref, wb_ref, o_ref, wa_bf, wb_bf, *, mode):
    @pl.when(pl.program_id(1) == 0)
    def _():
        wa_bf[...] = wa_ref[0].astype(BF16)
        wb_bf[...] = wb_ref[0].astype(BF16)

    for r0 in range(0, h_ref.shape[0], PROJ_DOT_ROWS):
        rows = slice(r0, r0 + PROJ_DOT_ROWS)
        h = h_ref[rows, :]
        a = jnp.dot(h, wa_bf[...], preferred_element_type=F32)
        b = jnp.dot(h, wb_bf[...], preferred_element_type=F32)
        if mode == "glu":
            y = a * jax.nn.sigmoid(b)
        else:
            y = a * jax.nn.sigmoid(a) * b
        o_ref[rows, :] = y.astype(o_ref.dtype)


def project_gated(h, wa, wb, n_cols, *, mode):
    M, K = h.shape
    bm, bn = PROJ_TILE_GATED
    assert M % bm == 0 and n_cols % bn == 0
    return pl.pallas_call(
        functools.partial(_proj_gated_kernel, mode=mode),
        grid=(n_cols // bn, M // bm),
        in_specs=[pl.BlockSpec((bm, K), lambda j, i: (i, 0)), _weight_spec(*wa, bn), _weight_spec(*wb, bn)],
        out_specs=pl.BlockSpec((bm, bn), lambda j, i: (i, j)),
        out_shape=jax.ShapeDtypeStruct((M, n_cols), BF16),
        scratch_shapes=[pltpu.VMEM((K, bn), BF16), pltpu.VMEM((K, bn), BF16)],
        compiler_params=_params(2),
        name="project_" + mode,
    )(h, wa[0], wb[0])


MEM_KV_COLS = 256


def _mem_kv_kernel(mem_ref, g_ref, wk_ref, wv_ref, k_ref, v_ref, memn_ref, *, rows):
    @pl.when(pl.program_id(0) == 0)
    def _():
        _rms_rows_to_bf16(mem_ref, g_ref, memn_ref, rows)

    memn = memn_ref[...]
    k_ref[...] = jnp.dot(memn, wk_ref[0].astype(BF16), preferred_element_type=F32).astype(BF16)
    v_ref[...] = jnp.dot(memn, wv_ref[0].astype(BF16), preferred_element_type=F32).astype(BF16)


def memory_kv(mem, g, wk_stack, wv_stack, layer):
    R, K = mem.shape
    N = wk_stack.shape[2]
    bn = MEM_KV_COLS
    assert N % bn == 0 and R % NORM_CHUNK == 0
    w_spec = pl.BlockSpec((pl.Element(1), pl.Element(K), pl.Element(bn)),
                          lambda j: (layer, 0, pl.multiple_of(j * bn, LANES)))
    out_spec = pl.BlockSpec((R, bn), lambda j: (0, j))
    out = jax.ShapeDtypeStruct((R, N), BF16)
    return pl.pallas_call(
        functools.partial(_mem_kv_kernel, rows=R),
        grid=(N // bn,),
        in_specs=[_resident((R, K)), _resident((1, K)), w_spec, w_spec],
        out_specs=(out_spec, out_spec),
        out_shape=(out, out),
        scratch_shapes=[pltpu.VMEM((R, K), BF16)],
        compiler_params=_params(1),
        name="memory_kv",
    )(mem, g.reshape(1, K), wk_stack, wv_stack)


def _down_kernel(*refs, layer, emit_next):
    if emit_next:
        a_ref, w_hbm, x_ref, g_ref, gn_ref, o_ref, hn_ref, w_ref, stage_ref, sem = refs
    else:
        a_ref, w_hbm, x_ref, g_ref, o_ref, w_ref, stage_ref, sem = refs

    @pl.when(pl.program_id(0) == 0)
    def _():
        _load_weight_bf16(w_hbm, layer, w_ref, stage_ref, sem)

    y = jnp.dot(a_ref[...], w_ref[...], preferred_element_type=F32)
    xn = x_ref[...] + _rms(y, g_ref[...])
    o_ref[...] = xn
    if emit_next:
        hn_ref[...] = _rms(xn, gn_ref[...]).astype(BF16)


def down_residual(a, w_stack, layer, x, g, g_next):
    M, K = a.shape
    N = w_stack.shape[2]
    bm = ROW_TILE
    emit_next = g_next is not None
    assert M % bm == 0
    tile = pl.BlockSpec((bm, N), lambda i: (i, 0))
    in_specs = [pl.BlockSpec((bm, K), lambda i: (i, 0)), _HBM, tile, _resident((1, N))]
    args = [a, w_stack, x, g.reshape(1, N)]
    out_specs, out_shape = tile, jax.ShapeDtypeStruct((M, N), F32)
    if emit_next:
        in_specs.append(_resident((1, N)))
        args.append(g_next.reshape(1, N))
        out_specs, out_shape = (tile, tile), (out_shape, jax.ShapeDtypeStruct((M, N), BF16))
    res = pl.pallas_call(
        functools.partial(_down_kernel, layer=layer, emit_next=emit_next),
        grid=(M // bm,),
        in_specs=in_specs,
        out_specs=out_specs,
        out_shape=out_shape,
        scratch_shapes=[pltpu.VMEM((K, N), BF16)] + _stage_scratch(N),
        compiler_params=_params(1),
        name="down_residual",
    )(*args)
    return res if emit_next else (res, None)


CONV_ROWS = 128
CONV_LN_ROWS = 8
CONV_LN_INTERLEAVE = 16
CONV_DATA_OFF = 16


def _conv_kernel(h_ref, wdw_ref, bdw_ref, lng_ref, lnb_ref, o_ref, hp_ref, cv_ref, *, S, C, KW):
    pad = KW // 2
    off = CONV_DATA_OFF
    n_slab = C // LANES
    R = CONV_ROWS

    for j in range(n_slab):
        cols = slice(j * LANES, (j + 1) * LANES)
        hp_ref[j, 0:off, :] = jnp.zeros((off, LANES), F32)
        hp_ref[j, off + S:off + S + off, :] = jnp.zeros((off, LANES), F32)

        def copy(i, carry, j=j, cols=cols):
            r = pl.multiple_of(i * COPY_ROWS, COPY_ROWS)
            hp_ref[j, pl.ds(off + r, COPY_ROWS), :] = h_ref[pl.ds(r, COPY_ROWS), cols].astype(F32)
            return carry
        lax.fori_loop(0, S // COPY_ROWS, copy, 0)

        def block(i, carry, j=j, cols=cols):
            for phase in range(2):
                t0 = i * (2 * R) + phase
                acc = jnp.broadcast_to(bdw_ref[:, cols], (R, LANES))
                for k in range(KW):
                    taps = hp_ref[j, pl.ds(t0 + (off - pad + k), R, stride=2), :]
                    acc = acc + taps * wdw_ref[k:k + 1, cols]
                cv_ref[j, pl.ds(t0, R, stride=2), :] = acc
            return carry
        lax.fori_loop(0, S // (2 * R), block, 0)

    def norm(i, carry):
        for u in range(CONV_LN_INTERLEAVE):
            start = (i * CONV_LN_INTERLEAVE + u) * CONV_LN_ROWS
            rows = pl.ds(pl.multiple_of(start, CONV_LN_ROWS), CONV_LN_ROWS)
            xs = [cv_ref[j, rows, :] for j in range(n_slab)]
            mu = jnp.sum(sum(xs), axis=-1, keepdims=True) * (1.0 / C)
            xs = [x - mu for x in xs]
            var = jnp.sum(sum(x * x for x in xs), axis=-1, keepdims=True) * (1.0 / C)
            inv = lax.rsqrt(var + EPS)
            for j in range(n_slab):
                cols = slice(j * LANES, (j + 1) * LANES)
                y = xs[j] * inv * lng_ref[:, cols] + lnb_ref[:, cols]
                o_ref[rows, cols] = (y * jax.nn.sigmoid(y)).astype(o_ref.dtype)
        return carry
    lax.fori_loop(0, S // (CONV_LN_ROWS * CONV_LN_INTERLEAVE), norm, 0)


def conv_branch(h, w_dw, b_dw, ln_g, ln_b):
    B, S, C = h.shape
    KW = w_dw.shape[0]
    assert KW // 2 < CONV_DATA_OFF and S % (2 * CONV_ROWS) == 0 and C % LANES == 0
    row = lambda v: v.reshape(1, C)
    vec = pl.BlockSpec((1, C), lambda b: (0, 0))
    return pl.pallas_call(
        functools.partial(_conv_kernel, S=S, C=C, KW=KW),
        grid=(B,),
        in_specs=[
            pl.BlockSpec((None, S, C), lambda b: (b, 0, 0)),
            pl.BlockSpec((KW, C), lambda b: (0, 0)),
            vec, vec, vec,
        ],
        out_specs=pl.BlockSpec((None, S, C), lambda b: (b, 0, 0)),
        out_shape=jax.ShapeDtypeStruct((B, S, C), BF16),
        scratch_shapes=[pltpu.VMEM((C // LANES, S + 2 * CONV_DATA_OFF, LANES), F32),
                        pltpu.VMEM((C // LANES, S, LANES), F32)],
        compiler_params=_params(1),
        name="conv_branch",
    )(h, w_dw, row(b_dw), row(ln_g), row(ln_b))


ATTN_Q_ROWS = 128
ATTN_SPLIT_STRIDE = 4


def _attn_kernel(q0, k0, v0, q1, k1, v1, q2, k2, v2, o_ref,
                 qf, kf, vf, qt, kt, vt, qs, kp, vp, og0, og1, og2, lg0, lg1, lg2, bias_ref, s_buf, p_buf, rd_buf, *, S):
    T = ATTN_Q_ROWS
    HB = BAND_HALF
    W = T + 2 * HB
    D = HEAD_DIM
    n_blocks = S // T
    qkv = ((q0, k0, v0), (q1, k1, v1), (q2, k2, v2))
    outs = ((og0, lg0), (og1, lg1), (og2, lg2))
    slot_f = jnp.full((T, W), pl.program_id(1), jnp.int32).astype(F32)

    a_idx = lax.broadcasted_iota(jnp.int32, (T, W), 0)
    c_idx = lax.broadcasted_iota(jnp.int32, (T, W), 1)
    c_row = lax.broadcasted_iota(jnp.int32, (1, W), 1)

    for g, (_, d) in enumerate(DIL_GROUPS):
        L = S // d
        nb = L // T
        pad = HB if nb > 1 else 0
        P = L + 2 * pad
        Wg = T + 2 * pad
        q_ref, k_ref, v_ref = qkv[g]
        og, lg = outs[g]

        slope = jnp.exp2(-8.0 * (slot_f + (HEADS_PER_GROUP * g + 1.0)) / N_ATTN_HEADS)
        rel = jnp.abs(c_idx - pad - a_idx)
        bias = jnp.where(rel <= HB, -(slope * (d * rel).astype(F32)), NEG_INF)
        bias_ref[:, 0:Wg] = bias[:, 0:Wg]

        if pad:
            kp[0:HB, :] = jnp.zeros((HB, D), BF16)
            vp[0:HB, :] = jnp.zeros((HB, D), BF16)

            def clear(r, carry, L=L, P=P):
                r0 = pl.multiple_of(r * P + (HB + L), HB)
                kp[pl.ds(r0, 2 * HB), :] = jnp.zeros((2 * HB, D), BF16)
                vp[pl.ds(r0, 2 * HB), :] = jnp.zeros((2 * HB, D), BF16)
                return carry
            lax.fori_loop(0, d, clear, 0)

        if d == 1:
            def copy(i, carry, k_ref=k_ref, v_ref=v_ref):
                r0 = pl.multiple_of(i * COPY_ROWS, COPY_ROWS)
                kp[pl.ds(HB + r0, COPY_ROWS), :] = k_ref[pl.ds(r0, COPY_ROWS), :]
                vp[pl.ds(HB + r0, COPY_ROWS), :] = v_ref[pl.ds(r0, COPY_ROWS), :]
                return carry
            lax.fori_loop(0, S // COPY_ROWS, copy, 0)
            q_src = q_ref
        else:
            def widen(i, carry, q_ref=q_ref, k_ref=k_ref, v_ref=v_ref):
                r0 = pl.multiple_of(i * COPY_ROWS, COPY_ROWS)
                qf[pl.ds(r0, COPY_ROWS), :] = q_ref[pl.ds(r0, COPY_ROWS), :].astype(F32)
                kf[pl.ds(r0, COPY_ROWS), :] = k_ref[pl.ds(r0, COPY_ROWS), :].astype(F32)
                vf[pl.ds(r0, COPY_ROWS), :] = v_ref[pl.ds(r0, COPY_ROWS), :].astype(F32)
                return carry
            lax.fori_loop(0, S // COPY_ROWS, widen, 0)

            if d % (ATTN_SPLIT_STRIDE ** 2) == 0:
                s1 = ATTN_SPLIT_STRIDE
                s2 = d // s1
                part = S // s1
                for src_ref, tmp_ref in ((qf, qt), (kf, kt), (vf, vt)):
                    for r1 in range(s1):
                        tmp_ref[r1 * part:(r1 + 1) * part, :] = src_ref[pl.ds(r1, part, stride=s1), :]
                strided = (qt, kt, vt)
                start = lambda r, c: (r % s1) * part + r // s1 + s2 * c * T
                stride = s2
            else:
                strided = (qf, kf, vf)
                start = lambda r, c: r + d * c * T
                stride = d

            def gather(r, carry, L=L, P=P, pad=pad, strided=strided, start=start, stride=stride):
                for c in range(L // T):
                    src = pl.ds(start(r, c), T, stride=stride)
                    qs[pl.ds(pl.multiple_of(r * L + c * T, T), T), :] = strided[0][src, :].astype(BF16)
                    dst = pl.ds(pl.multiple_of(r * P + (pad + c * T), HB), T)
                    kp[dst, :] = strided[1][src, :].astype(BF16)
                    vp[dst, :] = strided[2][src, :].astype(BF16)
                return carry
            lax.fori_loop(0, d, gather, 0, unroll=2)
            q_src = qs

        def geometry(n):
            r, i = divmod(n, nb)
            win = slice(r * P + i * T, r * P + i * T + Wg)
            dst = slice(n * T, (n + 1) * T) if d == 1 else pl.ds(r + d * i * T, T, stride=d)
            return i, win, dst

        for n in range(n_blocks):
            i, win, _ = geometry(n)
            s = lax.dot_general(q_src[n * T:(n + 1) * T, :], kp[win, :], (((1,), (1,)), ((), ())),
                                preferred_element_type=F32)
            s = s + bias_ref[:, 0:Wg]
            kidx = c_row + (i * T - HB)
            if pad and i == 0:
                s = s + jnp.where(kidx >= 0, 0.0, NEG_INF)
            if pad and i == nb - 1:
                s = s + jnp.where(kidx < L, 0.0, NEG_INF)
            s_buf[n, :, 0:Wg] = s
        for n in range(n_blocks):
            _, _, dst = geometry(n)
            m = jnp.max(s_buf[n, :, 0:Wg], axis=-1, keepdims=True)
            p = jnp.exp(s_buf[n, :, 0:Wg] - m)
            den = jnp.sum(p, axis=-1, keepdims=True)
            p_buf[n, :, 0:Wg] = p.astype(BF16)
            rd_buf[n] = jnp.broadcast_to(1.0 / den, (T, D))
            lg[dst, :] = jnp.broadcast_to(m + jnp.log(den), (T, D))
        for n in range(n_blocks):
            _, win, dst = geometry(n)
            og[dst, :] = jnp.dot(p_buf[n, :, 0:Wg], vp[win, :], preferred_element_type=F32) * rd_buf[n]

    def mix(i, carry):
        r0 = pl.multiple_of(i * COPY_ROWS, COPY_ROWS)
        rows = pl.ds(r0, COPY_ROWS)
        l0, l1, l2 = lg0[rows, :], lg1[rows, :], lg2[rows, :]
        m = jnp.maximum(jnp.maximum(l0, l1), l2)
        w0, w1, w2 = jnp.exp(l0 - m), jnp.exp(l1 - m), jnp.exp(l2 - m)
        y = (w0 * og0[rows, :] + w1 * og1[rows, :] + w2 * og2[rows, :]) / (w0 + w1 + w2)
        o_ref[rows, :] = y.astype(o_ref.dtype)
        return carry
    lax.fori_loop(0, S // COPY_ROWS, mix, 0)


def dilated_mixture_attention(qkv):
    B, S, _ = qkv.shape
    D = HEAD_DIM
    T = ATTN_Q_ROWS
    assert all(S % (d * T) == 0 and w // (2 * d) == BAND_HALF for w, d in DIL_GROUPS)
    assert S % COPY_ROWS == 0
    max_d = max(d for _, d in DIL_GROUPS)

    def head_spec(part, g):
        return pl.BlockSpec((None, S, D),
                            lambda b, j: (b, 0, part * N_ATTN_HEADS + HEADS_PER_GROUP * g + j))
    in_specs = [head_spec(part, g) for g in range(len(DIL_GROUPS)) for part in range(3)]
    seq_f32 = pltpu.VMEM((S, D), F32)
    return pl.pallas_call(
        functools.partial(_attn_kernel, S=S),
        grid=(B, HEADS_PER_GROUP),
        in_specs=in_specs,
        out_specs=pl.BlockSpec((None, S, D), lambda b, j: (b, 0, j)),
        out_shape=jax.ShapeDtypeStruct((B, S, HEADS_PER_GROUP * D), BF16),
        scratch_shapes=[
            seq_f32, seq_f32, seq_f32,
            seq_f32, seq_f32, seq_f32,
            pltpu.VMEM((S, D), BF16),
            pltpu.VMEM((S + 2 * BAND_HALF * max_d + BAND_HALF, D), BF16),
            pltpu.VMEM((S + 2 * BAND_HALF * max_d + BAND_HALF, D), BF16),
            seq_f32, seq_f32, seq_f32,
            seq_f32, seq_f32, seq_f32,
            pltpu.VMEM((T, T + 2 * BAND_HALF), F32),
            pltpu.VMEM((S // T, T, T + 2 * BAND_HALF), F32),
            pltpu.VMEM((S // T, T, T + 2 * BAND_HALF), BF16),
            pltpu.VMEM((S // T, T, D), F32),
        ],
        compiler_params=_params(2),
        name="dilated_attention",
    )(*([qkv] * 9))


def _merge_kernel(c_ref, a_ref, ga_ref, gb_ref, x_ref, wc_hbm, wa_hbm, wo_hbm, g_ref, o_ref,
                  wc_ref, wa_ref, wo_ref, stage_ref, sem, *, layer):
    @pl.when(pl.program_id(0) == 0)
    def _():
        _load_weight_bf16(wc_hbm, layer, wc_ref, stage_ref, sem)
        _load_weight_bf16(wa_hbm, layer, wa_ref, stage_ref, sem)
        _load_weight_bf16(wo_hbm, layer, wo_ref, stage_ref, sem)

    for r0 in range(0, x_ref.shape[0], ROW_TILE):
        rows = slice(r0, r0 + ROW_TILE)
        yc = jnp.dot(c_ref[rows, :], wc_ref[...], preferred_element_type=F32)
        ya = jnp.dot(a_ref[rows, :], wa_ref[...], preferred_element_type=F32)
        merged = ga_ref[rows, :].astype(F32) * yc + gb_ref[rows, :].astype(F32) * ya
        y = jnp.dot(merged.astype(BF16), wo_ref[...], preferred_element_type=F32)
        o_ref[rows, :] = x_ref[rows, :] + _rms(y, g_ref[...])


def merge_out(c, a, gates, x, wc, wa, wo, layer, g):
    M, D = x.shape
    Cc, Ca = c.shape[1], a.shape[1]
    bm = ROW_TILE * ROW_TILES_PER_STEP
    assert M % bm == 0
    return pl.pallas_call(
        functools.partial(_merge_kernel, layer=layer),
        grid=(M // bm,),
        in_specs=[
            pl.BlockSpec((bm, Cc), lambda i: (i, 0)),
            pl.BlockSpec((bm, Ca), lambda i: (i, 0)),
            pl.BlockSpec((bm, D), lambda i: (i, 0)),
            pl.BlockSpec((bm, D), lambda i: (i, 1)),
            pl.BlockSpec((bm, D), lambda i: (i, 0)),
            _HBM, _HBM, _HBM, _resident((1, D)),
        ],
        out_specs=pl.BlockSpec((bm, D), lambda i: (i, 0)),
        out_shape=jax.ShapeDtypeStruct((M, D), F32),
        scratch_shapes=[pltpu.VMEM((Cc, D), BF16), pltpu.VMEM((Ca, D), BF16), pltpu.VMEM((D, D), BF16)]
        + _stage_scratch(D),
        compiler_params=_params(1),
        name="merge_out",
    )(c, a, gates, gates, x, wc, wa, wo, g.reshape(1, D))


def _cross_kernel(x_ref, k_ref, v_ref, wq_hbm, wo_hbm, gpre_ref, gpost_ref, gn_ref, o_ref, hn_ref,
                  wq_ref, wo_ref, stage_ref, sem, *, D, layer):
    dh = D // X_HEADS

    @pl.when((pl.program_id(0) == 0) & (pl.program_id(1) == 0))
    def _():
        _load_weight_bf16(wq_hbm, layer, wq_ref, stage_ref, sem)
        _load_weight_bf16(wo_hbm, layer, wo_ref, stage_ref, sem)

    heads = [slice(hd * dh, (hd + 1) * dh) for hd in range(X_HEADS)]
    for r0 in range(0, x_ref.shape[0], ROW_TILE):
        rows = slice(r0, r0 + ROW_TILE)
        h = _rms(x_ref[rows, :], gpre_ref[...]).astype(BF16)
        q = (jnp.dot(h, wq_ref[...], preferred_element_type=F32) * (dh ** -0.5)).astype(BF16)
        scores = [lax.dot_general(q[:, cols], k_ref[:, cols], (((1,), (1,)), ((), ())),
                                  preferred_element_type=F32) for cols in heads]
        probs = []
        for s in scores:
            p = jnp.exp(s - jnp.max(s, axis=-1, keepdims=True))
            probs.append((p * (1.0 / jnp.sum(p, axis=-1, keepdims=True))).astype(BF16))
        ctx = [jnp.dot(p, v_ref[:, cols], preferred_element_type=F32).astype(BF16)
               for p, cols in zip(probs, heads)]
        y = jnp.dot(jnp.concatenate(ctx, axis=-1), wo_ref[...], preferred_element_type=F32)
        xn = x_ref[rows, :] + _rms(y, gpost_ref[...])
        o_ref[rows, :] = xn
        hn_ref[rows, :] = _rms(xn, gn_ref[...]).astype(BF16)


def cross_attention_sublayer(x, kmem, vmem, wq, wo, layer, g_pre, g_post, g_next):
    B, S, D = x.shape
    Nm = kmem.shape[1]
    bm = ROW_TILE * ROW_TILES_PER_STEP
    assert S % bm == 0
    tile = pl.BlockSpec((None, bm, D), lambda b, i: (b, i, 0))
    mem_spec = pl.BlockSpec((None, Nm, D), lambda b, i: (b, 0, 0))
    vec = lambda g: g.reshape(1, D)
    return pl.pallas_call(
        functools.partial(_cross_kernel, D=D, layer=layer),
        grid=(B, S // bm),
        in_specs=[tile, mem_spec, mem_spec, _HBM, _HBM,
                  _resident((1, D)), _resident((1, D)), _resident((1, D))],
        out_specs=(tile, tile),
        out_shape=(jax.ShapeDtypeStruct((B, S, D), F32), jax.ShapeDtypeStruct((B, S, D), BF16)),
        scratch_shapes=[pltpu.VMEM((D, D), BF16), pltpu.VMEM((D, D), BF16)] + _stage_scratch(D),
        compiler_params=_params(2),
        name="cross_attention",
    )(x, kmem, vmem, wq, wo, vec(g_pre), vec(g_post), vec(g_next))


def kernel(x, mem, w_in, w_dw, b_dw, conv_ln_g, conv_ln_b, w_conv_out, w_attn_proj, w_o, g_mix_pre, g_mix_post, g_mem, w_cq, w_ck, w_cv, w_co, g_x_pre, g_x_post, w_ffn_gate, w_ffn_up, w_ffn_down, g_ffn_pre, g_ffn_post):
    B, S, D = x.shape
    depth = w_in.shape[0]
    Cc = w_dw.shape[-1]
    Wq = N_ATTN_HEADS * HEAD_DIM
    Nm = mem.shape[1]
    d_ff = w_ffn_gate.shape[-1]
    c0 = 2 * Cc
    c3 = c0 + 3 * Wq
    M = B * S

    q_scale = jnp.concatenate([jnp.full((Wq,), HEAD_DIM ** -0.5, F32), jnp.ones((2 * Wq,), F32)])
    mem2 = mem.reshape(B * Nm, D)
    x2 = x.reshape(M, D)
    h = rms_norm_bf16(x2, g_mix_pre[0])
    for l in range(depth):
        glu = project_gated(h, (w_in, l, 0), (w_in, l, Cc), Cc, mode="glu")
        qkv = project(h, w_in, l, c0, 3 * Wq, PROJ_TILE_QKV, act="colscale", colscale=q_scale)
        gates = project(h, w_in, l, c3, 2 * D, PROJ_TILE_WIDE, act="sigmoid")
        conv = conv_branch(glu.reshape(B, S, Cc), w_dw[l], b_dw[l], conv_ln_g[l], conv_ln_b[l])
        attn = dilated_mixture_attention(qkv.reshape(B, S, 3 * Wq))
        x2 = merge_out(conv.reshape(M, Cc), attn.reshape(M, -1), gates, x2,
                       w_conv_out, w_attn_proj, w_o, l, g_mix_post[l])
        kmem, vmem = memory_kv(mem2, g_mem[l], w_ck, w_cv, l)
        x3, h3 = cross_attention_sublayer(x2.reshape(B, S, D), kmem.reshape(B, Nm, D), vmem.reshape(B, Nm, D),
                                          w_cq, w_co, l, g_x_pre[l], g_x_post[l], g_ffn_pre[l])
        x2, h = x3.reshape(M, D), h3.reshape(M, D)
        act = project_gated(h, (w_ffn_gate, l, 0), (w_ffn_up, l, 0), d_ff, mode="swiglu")
        x2, h = down_residual(act, w_ffn_down, l, x2, g_ffn_post[l],
                              g_mix_pre[l + 1] if l + 1 < depth else None)
    return x2.reshape(B, S, D)
```

```python
import functools

import jax
import jax.numpy as jnp
from jax import lax
from jax.experimental import pallas as pl
from jax.experimental.pallas import tpu as pltpu

F32 = jnp.float32
BF16 = jnp.bfloat16

EPS = 1e-6
NEG_INF = -1e30

HEAD_DIM = 128
DIL_GROUPS = ((128, 1), (512, 4), (2048, 16))
HEADS_PER_GROUP = 4
N_ATTN_HEADS = HEADS_PER_GROUP * len(DIL_GROUPS)
BAND_HALF = 64
X_HEADS = 4

V7X_VMEM_LIMIT_BYTES = 56 * 1024 * 1024
LANES = 128

PROJ_TILE_GATED = (2048, 512)
PROJ_TILE_QKV = (2048, 1152)
PROJ_TILE_WIDE = (2048, 1024)
PROJ_DOT_ROWS = 1024
ROW_TILE = 256
ROW_TILES_PER_STEP = 2
NORM_ROWS = 1024
NORM_CHUNK = 64
COPY_ROWS = 256


def _params(n_grid_dims):
    return pltpu.CompilerParams(
        dimension_semantics=("arbitrary",) * n_grid_dims,
        vmem_limit_bytes=V7X_VMEM_LIMIT_BYTES)


def _resident(shape):
    return pl.BlockSpec(shape, lambda *_: (0,) * len(shape), pipeline_mode=pl.Buffered(1))


def _rms(y, g):
    ms = jnp.mean(y * y, axis=-1, keepdims=True)
    return y * lax.rsqrt(ms + EPS) * g


WEIGHT_CHUNK_ROWS = 256
_HBM = pl.BlockSpec(memory_space=pl.ANY)


def _stage_scratch(n_cols):
    return [pltpu.VMEM((2, WEIGHT_CHUNK_ROWS, n_cols), F32), pltpu.SemaphoreType.DMA((2,))]


def _load_weight_bf16(w_hbm, layer, wb_ref, stage_ref, sem):
    R = WEIGHT_CHUNK_ROWS
    n_chunks = wb_ref.shape[0] // R
    assert wb_ref.shape[0] % R == 0

    def copy(c):
        return pltpu.make_async_copy(w_hbm.at[layer, pl.ds(c * R, R), :], stage_ref.at[c % 2], sem.at[c % 2])

    copy(0).start()
    for c in range(n_chunks):
        if c + 1 < n_chunks:
            copy(c + 1).start()
        copy(c).wait()
        wb_ref[c * R:(c + 1) * R, :] = stage_ref[c % 2].astype(BF16)


def _rms_rows_to_bf16(x_ref, g_ref, o_ref, rows):
    def body(i, carry):
        r = pl.multiple_of(i * NORM_CHUNK, NORM_CHUNK)
        o_ref[pl.ds(r, NORM_CHUNK), :] = _rms(x_ref[pl.ds(r, NORM_CHUNK), :], g_ref[...]).astype(BF16)
        return carry
    lax.fori_loop(0, rows // NORM_CHUNK, body, 0)


def _norm_kernel(x_ref, g_ref, o_ref, *, bm):
    _rms_rows_to_bf16(x_ref, g_ref, o_ref, bm)


def rms_norm_bf16(x, g):
    M, K = x.shape
    bm = NORM_ROWS
    assert M % bm == 0
    return pl.pallas_call(
        functools.partial(_norm_kernel, bm=bm),
        grid=(M // bm,),
        in_specs=[pl.BlockSpec((bm, K), lambda i: (i, 0)), pl.BlockSpec((1, K), lambda i: (0, 0))],
        out_specs=pl.BlockSpec((bm, K), lambda i: (i, 0)),
        out_shape=jax.ShapeDtypeStruct((M, K), BF16),
        compiler_params=_params(1),
        name="rms_norm",
    )(x, g.reshape(1, K))


def _weight_spec(w_stack, layer, col0, bn):
    K = w_stack.shape[1]
    assert col0 % LANES == 0 and bn % LANES == 0
    return pl.BlockSpec((pl.Element(1), pl.Element(K), pl.Element(bn)),
                        lambda j, i: (layer, 0, pl.multiple_of(col0 + j * bn, LANES)))


def _proj_kernel(*refs, act):
    if act == "colscale":
        h_ref, w_ref, cs_ref, o_ref, wb_ref = refs
    else:
        h_ref, w_ref, o_ref, wb_ref = refs

    @pl.when(pl.program_id(1) == 0)
    def _():
        wb_ref[...] = w_ref[0].astype(BF16)

    for r0 in range(0, h_ref.shape[0], PROJ_DOT_ROWS):
        rows = slice(r0, r0 + PROJ_DOT_ROWS)
        acc = jnp.dot(h_ref[rows, :], wb_ref[...], preferred_element_type=F32)
        if act == "sigmoid":
            acc = jax.nn.sigmoid(acc)
        elif act == "colscale":
            acc = acc * cs_ref[...]
        o_ref[rows, :] = acc.astype(o_ref.dtype)


def project(h, w_stack, layer, col0, n_cols, tile, *, act, colscale=None):
    M, K = h.shape
    bm, bn = tile
    assert M % bm == 0 and n_cols % bn == 0 and (act == "colscale") == (colscale is not None)
    in_specs = [pl.BlockSpec((bm, K), lambda j, i: (i, 0)), _weight_spec(w_stack, layer, col0, bn)]
    args = [h, w_stack]
    if colscale is not None:
        in_specs.append(pl.BlockSpec((1, bn), lambda j, i: (0, j)))
        args.append(colscale.reshape(1, n_cols))
    return pl.pallas_call(
        functools.partial(_proj_kernel, act=act),
        grid=(n_cols // bn, M // bm),
        in_specs=in_specs,
        out_specs=pl.BlockSpec((bm, bn), lambda j, i: (i, j)),
        out_shape=jax.ShapeDtypeStruct((M, n_cols), BF16),
        scratch_shapes=[pltpu.VMEM((K, bn), BF16)],
        compiler_params=_params(2),
        name="project_" + act,
    )(*args)


def _proj_gated_kernel(h_ref, wa_ref, wb_ref, o_ref, wa_bf, wb_bf, *, mode):
    @pl.when(pl.program_id(1) == 0)
    def _():
        wa_bf[...] = wa_ref[0].astype(BF16)
        wb_bf[...] = wb_ref[0].astype(BF16)

    for r0 in range(0, h_ref.shape[0], PROJ_DOT_ROWS):
        rows = slice(r0, r0 + PROJ_DOT_ROWS)
        h = h_ref[rows, :]
        a = jnp.dot(h, wa_bf[...], preferred_element_type=F32)
        b = jnp.dot(h, wb_bf[...], preferred_element_type=F32)
        if mode == "glu":
            y = a * jax.nn.sigmoid(b)
        else:
            y = a * jax.nn.sigmoid(a) * b
        o_ref[rows, :] = y.astype(o_ref.dtype)


def project_gated(h, wa, wb, n_cols, *, mode):
    M, K = h.shape
    bm, bn = PROJ_TILE_GATED
    assert M % bm == 0 and n_cols % bn == 0
    return pl.pallas_call(
        functools.partial(_proj_gated_kernel, mode=mode),
        grid=(n_cols // bn, M // bm),
        in_specs=[pl.BlockSpec((bm, K), lambda j, i: (i, 0)), _weight_spec(*wa, bn), _weight_spec(*wb, bn)],
        out_specs=pl.BlockSpec((bm, bn), lambda j, i: (i, j)),
        out_shape=jax.ShapeDtypeStruct((M, n_cols), BF16),
        scratch_shapes=[pltpu.VMEM((K, bn), BF16), pltpu.VMEM((K, bn), BF16)],
        compiler_params=_params(2),
        name="project_" + mode,
    )(h, wa[0], wb[0])


MEM_KV_COLS = 256


def _mem_kv_kernel(mem_ref, g_ref, wk_ref, wv_ref, k_ref, v_ref, memn_ref, *, rows):
    @pl.when(pl.program_id(0) == 0)
    def _():
        _rms_rows_to_bf16(mem_ref, g_ref, memn_ref, rows)

    memn = memn_ref[...]
    k_ref[...] = jnp.dot(memn, wk_ref[0].astype(BF16), preferred_element_type=F32).astype(BF16)
    v_ref[...] = jnp.dot(memn, wv_ref[0].astype(BF16), preferred_element_type=F32).astype(BF16)


def memory_kv(mem, g, wk_stack, wv_stack, layer):
    R, K = mem.shape
    N = wk_stack.shape[2]
    bn = MEM_KV_COLS
    assert N % bn == 0 and R % NORM_CHUNK == 0
    w_spec = pl.BlockSpec((pl.Element(1), pl.Element(K), pl.Element(bn)),
                          lambda j: (layer, 0, pl.multiple_of(j * bn, LANES)))
    out_spec = pl.BlockSpec((R, bn), lambda j: (0, j))
    out = jax.ShapeDtypeStruct((R, N), BF16)
    return pl.pallas_call(
        functools.partial(_mem_kv_kernel, rows=R),
        grid=(N // bn,),
        in_specs=[_resident((R, K)), _resident((1, K)), w_spec, w_spec],
        out_specs=(out_spec, out_spec),
        out_shape=(out, out),
        scratch_shapes=[pltpu.VMEM((R, K), BF16)],
        compiler_params=_params(1),
        name="memory_kv",
    )(mem, g.reshape(1, K), wk_stack, wv_stack)


def _down_kernel(*refs, layer, emit_next):
    if emit_next:
        a_ref, w_hbm, x_ref, g_ref, gn_ref, o_ref, hn_ref, w_ref, stage_ref, sem = refs
    else:
        a_ref, w_hbm, x_ref, g_ref, o_ref, w_ref, stage_ref, sem = refs

    @pl.when(pl.program_id(0) == 0)
    def _():
        _load_weight_bf16(w_hbm, layer, w_ref, stage_ref, sem)

    y = jnp.dot(a_ref[...], w_ref[...], preferred_element_type=F32)
    xn = x_ref[...] + _rms(y, g_ref[...])
    o_ref[...] = xn
    if emit_next:
        hn_ref[...] = _rms(xn, gn_ref[...]).astype(BF16)


def down_residual(a, w_stack, layer, x, g, g_next):
    M, K = a.shape
    N = w_stack.shape[2]
    bm = ROW_TILE
    emit_next = g_next is not None
    assert M % bm == 0
    tile = pl.BlockSpec((bm, N), lambda i: (i, 0))
    in_specs = [pl.BlockSpec((bm, K), lambda i: (i, 0)), _HBM, tile, _resident((1, N))]
    args = [a, w_stack, x, g.reshape(1, N)]
    out_specs, out_shape = tile, jax.ShapeDtypeStruct((M, N), F32)
    if emit_next:
        in_specs.append(_resident((1, N)))
        args.append(g_next.reshape(1, N))
        out_specs, out_shape = (tile, tile), (out_shape, jax.ShapeDtypeStruct((M, N), BF16))
    res = pl.pallas_call(
        functools.partial(_down_kernel, layer=layer, emit_next=emit_next),
        grid=(M // bm,),
        in_specs=in_specs,
        out_specs=out_specs,
        out_shape=out_shape,
        scratch_shapes=[pltpu.VMEM((K, N), BF16)] + _stage_scratch(N),
        compiler_params=_params(1),
        name="down_residual",
    )(*args)
    return res if emit_next else (res, None)


CONV_ROWS = 128
CONV_LN_ROWS = 8
CONV_LN_INTERLEAVE = 16
CONV_DATA_OFF = 16


def _conv_kernel(h_ref, wdw_ref, bdw_ref, lng_ref, lnb_ref, o_ref, hp_ref, cv_ref, *, S, C, KW):
    pad = KW // 2
    off = CONV_DATA_OFF
    n_slab = C // LANES
    R = CONV_ROWS

    for j in range(n_slab):
        cols = slice(j * LANES, (j + 1) * LANES)
        hp_ref[j, 0:off, :] = jnp.zeros((off, LANES), F32)
        hp_ref[j, off + S:off + S + off, :] = jnp.zeros((off, LANES), F32)

        def copy(i, carry, j=j, cols=cols):
            r = pl.multiple_of(i * COPY_ROWS, COPY_ROWS)
            hp_ref[j, pl.ds(off + r, COPY_ROWS), :] = h_ref[pl.ds(r, COPY_ROWS), cols].astype(F32)
            return carry
        lax.fori_loop(0, S // COPY_ROWS, copy, 0)

        def block(i, carry, j=j, cols=cols):
            for phase in range(2):
                t0 = i * (2 * R) + phase
                acc = jnp.broadcast_to(bdw_ref[:, cols], (R, LANES))
                for k in range(KW):
                    taps = hp_ref[j, pl.ds(t0 + (off - pad + k), R, stride=2), :]
                    acc = acc + taps * wdw_ref[k:k + 1, cols]
                cv_ref[j, pl.ds(t0, R, stride=2), :] = acc
            return carry
        lax.fori_loop(0, S // (2 * R), block, 0)

    def norm(i, carry):
        for u in range(CONV_LN_INTERLEAVE):
            start = (i * CONV_LN_INTERLEAVE + u) * CONV_LN_ROWS
            rows = pl.ds(pl.multiple_of(start, CONV_LN_ROWS), CONV_LN_ROWS)
            xs = [cv_ref[j, rows, :] for j in range(n_slab)]
            mu = jnp.sum(sum(xs), axis=-1, keepdims=True) * (1.0 / C)
            xs = [x - mu for x in xs]
            var = jnp.sum(sum(x * x for x in xs), axis=-1, keepdims=True) * (1.0 / C)
            inv = lax.rsqrt(var + EPS)
            for j in range(n_slab):
                cols = slice(j * LANES, (j + 1) * LANES)
                y = xs[j] * inv * lng_ref[:, cols] + lnb_ref[:, cols]
                o_ref[rows, cols] = (y * jax.nn.sigmoid(y)).astype(o_ref.dtype)
        return carry
    lax.fori_loop(0, S // (CONV_LN_ROWS * CONV_LN_INTERLEAVE), norm, 0)


def conv_branch(h, w_dw, b_dw, ln_g, ln_b):
    B, S, C = h.shape
    KW = w_dw.shape[0]
    assert KW // 2 < CONV_DATA_OFF and S % (2 * CONV_ROWS) == 0 and C % LANES == 0
    row = lambda v: v.reshape(1, C)
    vec = pl.BlockSpec((1, C), lambda b: (0, 0))
    return pl.pallas_call(
        functools.partial(_conv_kernel, S=S, C=C, KW=KW),
        grid=(B,),
        in_specs=[
            pl.BlockSpec((None, S, C), lambda b: (b, 0, 0)),
            pl.BlockSpec((KW, C), lambda b: (0, 0)),
            vec, vec, vec,
        ],
        out_specs=pl.BlockSpec((None, S, C), lambda b: (b, 0, 0)),
        out_shape=jax.ShapeDtypeStruct((B, S, C), BF16),
        scratch_shapes=[pltpu.VMEM((C // LANES, S + 2 * CONV_DATA_OFF, LANES), F32),
                        pltpu.VMEM((C // LANES, S, LANES), F32)],
        compiler_params=_params(1),
        name="conv_branch",
    )(h, w_dw, row(b_dw), row(ln_g), row(ln_b))


ATTN_Q_ROWS = 128
ATTN_SPLIT_STRIDE = 4


def _attn_kernel(q0, k0, v0, q1, k1, v1, q2, k2, v2, o_ref,
                 qf, kf, vf, qt, kt, vt, qs, kp, vp, og0, og1, og2, lg0, lg1, lg2, bias_ref, s_buf, p_buf, rd_buf, *, S):
    T = ATTN_Q_ROWS
    HB = BAND_HALF
    W = T + 2 * HB
    D = HEAD_DIM
    n_blocks = S // T
    qkv = ((q0, k0, v0), (q1, k1, v1), (q2, k2, v2))
    outs = ((og0, lg0), (og1, lg1), (og2, lg2))
    slot_f = jnp.full((T, W), pl.program_id(1), jnp.int32).astype(F32)

    a_idx = lax.broadcasted_iota(jnp.int32, (T, W), 0)
    c_idx = lax.broadcasted_iota(jnp.int32, (T, W), 1)
    c_row = lax.broadcasted_iota(jnp.int32, (1, W), 1)

    for g, (_, d) in enumerate(DIL_GROUPS):
        L = S // d
        nb = L // T
        pad = HB if nb > 1 else 0
        P = L + 2 * pad
        Wg = T + 2 * pad
        q_ref, k_ref, v_ref = qkv[g]
        og, lg = outs[g]

        slope = jnp.exp2(-8.0 * (slot_f + (HEADS_PER_GROUP * g + 1.0)) / N_ATTN_HEADS)
        rel = jnp.abs(c_idx - pad - a_idx)
        bias = jnp.where(rel <= HB, -(slope * (d * rel).astype(F32)), NEG_INF)
        bias_ref[:, 0:Wg] = bias[:, 0:Wg]

        if pad:
            kp[0:HB, :] = jnp.zeros((HB, D), BF16)
            vp[0:HB, :] = jnp.zeros((HB, D), BF16)

            def clear(r, carry, L=L, P=P):
                r0 = pl.multiple_of(r * P + (HB + L), HB)
                kp[pl.ds(r0, 2 * HB), :] = jnp.zeros((2 * HB, D), BF16)
                vp[pl.ds(r0, 2 * HB), :] = jnp.zeros((2 * HB, D), BF16)
                return carry
            lax.fori_loop(0, d, clear, 0)

        if d == 1:
            def copy(i, carry, k_ref=k_ref, v_ref=v_ref):
                r0 = pl.multiple_of(i * COPY_ROWS, COPY_ROWS)
                kp[pl.ds(HB + r0, COPY_ROWS), :] = k_ref[pl.ds(r0, COPY_ROWS), :]
                vp[pl.ds(HB + r0, COPY_ROWS), :] = v_ref[pl.ds(r0, COPY_ROWS), :]
                return carry
            lax.fori_loop(0, S // COPY_ROWS, copy, 0)
            q_src = q_ref
        else:
            def widen(i, carry, q_ref=q_ref, k_ref=k_ref, v_ref=v_ref):
                r0 = pl.multiple_of(i * COPY_ROWS, COPY_ROWS)
                qf[pl.ds(r0, COPY_ROWS), :] = q_ref[pl.ds(r0, COPY_ROWS), :].astype(F32)
                kf[pl.ds(r0, COPY_ROWS), :] = k_ref[pl.ds(r0, COPY_ROWS), :].astype(F32)
                vf[pl.ds(r0, COPY_ROWS), :] = v_ref[pl.ds(r0, COPY_ROWS), :].astype(F32)
                return carry
            lax.fori_loop(0, S // COPY_ROWS, widen, 0)

            if d % (ATTN_SPLIT_STRIDE ** 2) == 0:
                s1 = ATTN_SPLIT_STRIDE
                s2 = d // s1
                part = S // s1
                for src_ref, tmp_ref in ((qf, qt), (kf, kt), (vf, vt)):
                    for r1 in range(s1):
                        tmp_ref[r1 * part:(r1 + 1) * part, :] = src_ref[pl.ds(r1, part, stride=s1), :]
                strided = (qt, kt, vt)
                start = lambda r, c: (r % s1) * part + r // s1 + s2 * c * T
                stride = s2
            else:
                strided = (qf, kf, vf)
                start = lambda r, c: r + d * c * T
                stride = d

            def gather(r, carry, L=L, P=P, pad=pad, strided=strided, start=start, stride=stride):
                for c in range(L // T):
                    src = pl.ds(start(r, c), T, stride=stride)
                    qs[pl.ds(pl.multiple_of(r * L + c * T, T), T), :] = strided[0][src, :].astype(BF16)
                    dst = pl.ds(pl.multiple_of(r * P + (pad + c * T), HB), T)
                    kp[dst, :] = strided[1][src, :].astype(BF16)
                    vp[dst, :] = strided[2][src, :].astype(BF16)
                return carry
            lax.fori_loop(0, d, gather, 0, unroll=2)
            q_src = qs

        def geometry(n):
            r, i = divmod(n, nb)
            win = slice(r * P + i * T, r * P + i * T + Wg)
            dst = slice(n * T, (n + 1) * T) if d == 1 else pl.ds(r + d * i * T, T, stride=d)
            return i, win, dst

        for n in range(n_blocks):
            i, win, _ = geometry(n)
            s = lax.dot_general(q_src[n * T:(n + 1) * T, :], kp[win, :], (((1,), (1,)), ((), ())),
                                preferred_element_type=F32)
            s = s + bias_ref[:, 0:Wg]
            kidx = c_row + (i * T - HB)
            if pad and i == 0:
                s = s + jnp.where(kidx >= 0, 0.0, NEG_INF)
            if pad and i == nb - 1:
                s = s + jnp.where(kidx < L, 0.0, NEG_INF)
            s_buf[n, :, 0:Wg] = s
        for n in range(n_blocks):
            _, _, dst = geometry(n)
            m = jnp.max(s_buf[n, :, 0:Wg], axis=-1, keepdims=True)
            p = jnp.exp(s_buf[n, :, 0:Wg] - m)
            den = jnp.sum(p, axis=-1, keepdims=True)
            p_buf[n, :, 0:Wg] = p.astype(BF16)
            rd_buf[n] = jnp.broadcast_to(1.0 / den, (T, D))
            lg[dst, :] = jnp.broadcast_to(m + jnp.log(den), (T, D))
        for n in range(n_blocks):
            _, win, dst = geometry(n)
            og[dst, :] = jnp.dot(p_buf[n, :, 0:Wg], vp[win, :], preferred_element_type=F32) * rd_buf[n]

    def mix(i, carry):
        r0 = pl.multiple_of(i * COPY_ROWS, COPY_ROWS)
        rows = pl.ds(r0, COPY_ROWS)
        l0, l1, l2 = lg0[rows, :], lg1[rows, :], lg2[rows, :]
        m = jnp.maximum(jnp.maximum(l0, l1), l2)
        w0, w1, w2 = jnp.exp(l0 - m), jnp.exp(l1 - m), jnp.exp(l2 - m)
        y = (w0 * og0[rows, :] + w1 * og1[rows, :] + w2 * og2[rows, :]) / (w0 + w1 + w2)
        o_ref[rows, :] = y.astype(o_ref.dtype)
        return carry
    lax.fori_loop(0, S // COPY_ROWS, mix, 0)


def dilated_mixture_attention(qkv):
    B, S, _ = qkv.shape
    D = HEAD_DIM
    T = ATTN_Q_ROWS
    assert all(S % (d * T) == 0 and w // (2 * d) == BAND_HALF for w, d in DIL_GROUPS)
    assert S % COPY_ROWS == 0
    max_d = max(d for _, d in DIL_GROUPS)

    def head_spec(part, g):
        return pl.BlockSpec((None, S, D),
                            lambda b, j: (b, 0, part * N_ATTN_HEADS + HEADS_PER_GROUP * g + j))
    in_specs = [head_spec(part, g) for g in range(len(DIL_GROUPS)) for part in range(3)]
    seq_f32 = pltpu.VMEM((S, D), F32)
    return pl.pallas_call(
        functools.partial(_attn_kernel, S=S),
        grid=(B, HEADS_PER_GROUP),
        in_specs=in_specs,
        out_specs=pl.BlockSpec((None, S, D), lambda b, j: (b, 0, j)),
        out_shape=jax.ShapeDtypeStruct((B, S, HEADS_PER_GROUP * D), BF16),
        scratch_shapes=[
            seq_f32, seq_f32, seq_f32,
            seq_f32, seq_f32, seq_f32,
            pltpu.VMEM((S, D), BF16),
            pltpu.VMEM((S + 2 * BAND_HALF * max_d + BAND_HALF, D), BF16),
            pltpu.VMEM((S + 2 * BAND_HALF * max_d + BAND_HALF, D), BF16),
            seq_f32, seq_f32, seq_f32,
            seq_f32, seq_f32, seq_f32,
            pltpu.VMEM((T, T + 2 * BAND_HALF), F32),
            pltpu.VMEM((S // T, T, T + 2 * BAND_HALF), F32),
            pltpu.VMEM((S // T, T, T + 2 * BAND_HALF), BF16),
            pltpu.VMEM((S // T, T, D), F32),
        ],
        compiler_params=_params(2),
        name="dilated_attention",
    )(*([qkv] * 9))


def _merge_kernel(c_ref, a_ref, ga_ref, gb_ref, x_ref, wc_hbm, wa_hbm, wo_hbm, g_ref, o_ref,
                  wc_ref, wa_ref, wo_ref, stage_ref, sem, *, layer):
    @pl.when(pl.program_id(0) == 0)
    def _():
        _load_weight_bf16(wc_hbm, layer, wc_ref, stage_ref, sem)
        _load_weight_bf16(wa_hbm, layer, wa_ref, stage_ref, sem)
        _load_weight_bf16(wo_hbm, layer, wo_ref, stage_ref, sem)

    for r0 in range(0, x_ref.shape[0], ROW_TILE):
        rows = slice(r0, r0 + ROW_TILE)
        yc = jnp.dot(c_ref[rows, :], wc_ref[...], preferred_element_type=F32)
        ya = jnp.dot(a_ref[rows, :], wa_ref[...], preferred_element_type=F32)
        merged = ga_ref[rows, :].astype(F32) * yc + gb_ref[rows, :].astype(F32) * ya
        y = jnp.dot(merged.astype(BF16), wo_ref[...], preferred_element_type=F32)
        o_ref[rows, :] = x_ref[rows, :] + _rms(y, g_ref[...])


def merge_out(c, a, gates, x, wc, wa, wo, layer, g):
    M, D = x.shape
    Cc, Ca = c.shape[1], a.shape[1]
    bm = ROW_TILE * ROW_TILES_PER_STEP
    assert M % bm == 0
    return pl.pallas_call(
        functools.partial(_merge_kernel, layer=layer),
        grid=(M // bm,),
        in_specs=[
            pl.BlockSpec((bm, Cc), lambda i: (i, 0)),
            pl.BlockSpec((bm, Ca), lambda i: (i, 0)),
            pl.BlockSpec((bm, D), lambda i: (i, 0)),
            pl.BlockSpec((bm, D), lambda i: (i, 1)),
            pl.BlockSpec((bm, D), lambda i: (i, 0)),
            _HBM, _HBM, _HBM, _resident((1, D)),
        ],
        out_specs=pl.BlockSpec((bm, D), lambda i: (i, 0)),
        out_shape=jax.ShapeDtypeStruct((M, D), F32),
        scratch_shapes=[pltpu.VMEM((Cc, D), BF16), pltpu.VMEM((Ca, D), BF16), pltpu.VMEM((D, D), BF16)]
        + _stage_scratch(D),
        compiler_params=_params(1),
        name="merge_out",
    )(c, a, gates, gates, x, wc, wa, wo, g.reshape(1, D))


def _cross_kernel(x_ref, k_ref, v_ref, wq_hbm, wo_hbm, gpre_ref, gpost_ref, gn_ref, o_ref, hn_ref,
                  wq_ref, wo_ref, stage_ref, sem, *, D, layer):
    dh = D // X_HEADS

    @pl.when((pl.program_id(0) == 0) & (pl.program_id(1) == 0))
    def _():
        _load_weight_bf16(wq_hbm, layer, wq_ref, stage_ref, sem)
        _load_weight_bf16(wo_hbm, layer, wo_ref, stage_ref, sem)

    heads = [slice(hd * dh, (hd + 1) * dh) for hd in range(X_HEADS)]
    for r0 in range(0, x_ref.shape[0], ROW_TILE):
        rows = slice(r0, r0 + ROW_TILE)
        h = _rms(x_ref[rows, :], gpre_ref[...]).astype(BF16)
        q = (jnp.dot(h, wq_ref[...], preferred_element_type=F32) * (dh ** -0.5)).astype(BF16)
        scores = [lax.dot_general(q[:, cols], k_ref[:, cols], (((1,), (1,)), ((), ())),
                                  preferred_element_type=F32) for cols in heads]
        probs = []
        for s in scores:
            p = jnp.exp(s - jnp.max(s, axis=-1, keepdims=True))
            probs.append((p * (1.0 / jnp.sum(p, axis=-1, keepdims=True))).astype(BF16))
        ctx = [jnp.dot(p, v_ref[:, cols], preferred_element_type=F32).astype(BF16)
               for p, cols in zip(probs, heads)]
        y = jnp.dot(jnp.concatenate(ctx, axis=-1), wo_ref[...], preferred_element_type=F32)
        xn = x_ref[rows, :] + _rms(y, gpost_ref[...])
        o_ref[rows, :] = xn
        hn_ref[rows, :] = _rms(xn, gn_ref[...]).astype(BF16)


def cross_attention_sublayer(x, kmem, vmem, wq, wo, layer, g_pre, g_post, g_next):
    B, S, D = x.shape
    Nm = kmem.shape[1]
    bm = ROW_TILE * ROW_TILES_PER_STEP
    assert S % bm == 0
    tile = pl.BlockSpec((None, bm, D), lambda b, i: (b, i, 0))
    mem_spec = pl.BlockSpec((None, Nm, D), lambda b, i: (b, 0, 0))
    vec = lambda g: g.reshape(1, D)
    return pl.pallas_call(
        functools.partial(_cross_kernel, D=D, layer=layer),
        grid=(B, S // bm),
        in_specs=[tile, mem_spec, mem_spec, _HBM, _HBM,
                  _resident((1, D)), _resident((1, D)), _resident((1, D))],
        out_specs=(tile, tile),
        out_shape=(jax.ShapeDtypeStruct((B, S, D), F32), jax.ShapeDtypeStruct((B, S, D), BF16)),
        scratch_shapes=[pltpu.VMEM((D, D), BF16), pltpu.VMEM((D, D), BF16)] + _stage_scratch(D),
        compiler_params=_params(2),
        name="cross_attention",
    )(x, kmem, vmem, wq, wo, vec(g_pre), vec(g_post), vec(g_next))


def kernel(x, mem, w_in, w_dw, b_dw, conv_ln_g, conv_ln_b, w_conv_out, w_attn_proj, w_o, g_mix_pre, g_mix_post, g_mem, w_cq, w_ck, w_cv, w_co, g_x_pre, g_x_post, w_ffn_gate, w_ffn_up, w_ffn_down, g_ffn_pre, g_ffn_post):
    B, S, D = x.shape
    depth = w_in.shape[0]
    Cc = w_dw.shape[-1]
    Wq = N_ATTN_HEADS * HEAD_DIM
    Nm = mem.shape[1]
    d_ff = w_ffn_gate.shape[-1]
    c0 = 2 * Cc
    c3 = c0 + 3 * Wq
    M = B * S

    q_scale = jnp.concatenate([jnp.full((Wq,), HEAD_DIM ** -0.5, F32), jnp.ones((2 * Wq,), F32)])
    mem2 = mem.reshape(B * Nm, D)
    x2 = x.reshape(M, D)
    h = rms_norm_bf16(x2, g_mix_pre[0])
    for l in range(depth):
        glu = project_gated(h, (w_in, l, 0), (w_in, l, Cc), Cc, mode="glu")
        qkv = project(h, w_in, l, c0, 3 * Wq, PROJ_TILE_QKV, act="colscale", colscale=q_scale)
        gates = project(h, w_in, l, c3, 2 * D, PROJ_TILE_WIDE, act="sigmoid")
        conv = conv_branch(glu.reshape(B, S, Cc), w_dw[l], b_dw[l], conv_ln_g[l], conv_ln_b[l])
        attn = dilated_mixture_attention(qkv.reshape(B, S, 3 * Wq))
        x2 = merge_out(conv.reshape(M, Cc), attn.reshape(M, -1), gates, x2,
                       w_conv_out, w_attn_proj, w_o, l, g_mix_post[l])
        kmem, vmem = memory_kv(mem2, g_mem[l], w_ck, w_cv, l)
        x3, h3 = cross_attention_sublayer(x2.reshape(B, S, D), kmem.reshape(B, Nm, D), vmem.reshape(B, Nm, D),
                                          w_cq, w_co, l, g_x_pre[l], g_x_post[l], g_ffn_pre[l])
        x2, h = x3.reshape(M, D), h3.reshape(M, D)
        act = project_gated(h, (w_ffn_gate, l, 0), (w_ffn_up, l, 0), d_ff, mode="swiglu")
        x2, h = down_residual(act, w_ffn_down, l, x2, g_ffn_post[l],
                              g_mix_pre[l + 1] if l + 1 < depth else None)
    return x2.reshape(B, S, D)
```
